```python
import math
import jax
import jax.numpy as jnp
from jax import lax
import numpy as np

D_MODEL = 1024
BATCH = 4
SEQ = 8192
DEPTH = 2

CTX_LEN = 256
GRID_W = 64
D_MIX = D_MODEL
D_FF = 4 * D_MODEL
N_MOD = 6
EPS = 1e-6

HY_WIDTH = D_MIX // 4
HY_ORDER = 2
HY_SHORT = 3
HY_EMB = 33
HY_BANDS = (HY_EMB - 1) // 2
HY_FFN = 64
HY_TARGET = 1e-2
HY_FAST_PCT = 0.3
HY_SLOW_PCT = 1.5
HY_COLS = (HY_ORDER + 1) * HY_WIDTH
HY_FILT_OUT = 2 * HY_ORDER * HY_WIDTH

RET_DK = 64
RET_DV = 64
RET_HEADS = (D_MIX // 4) // RET_DV
RET_CHUNK = 128
RET_COLS = RET_HEADS * (2 * RET_DK + 2 * RET_DV)

MLA_V = 64
MLA_HEADS = (D_MIX // 2) // MLA_V
MLA_NOPE = 64
MLA_ROPE = 32
MLA_QK = MLA_NOPE + MLA_ROPE
MLA_Q_LORA = D_MODEL // 4
MLA_KV_LORA = D_MODEL // 8
MLA_COLS = MLA_Q_LORA + MLA_KV_LORA + MLA_ROPE
ROPE_BASE = 10000.0
ATTN_BLOCK = 128

IN_COLS = HY_COLS + RET_COLS + MLA_COLS

kernel_name = 'hybrid_hyena_retnet_mla_dit'


def rms_norm(x, gain):
    xf = x.astype(jnp.float32)
    y = xf * lax.rsqrt(jnp.mean(xf * xf, axis=-1, keepdims=True) + EPS)
    return (y * gain.astype(jnp.float32)).astype(x.dtype)


def modulate(x, gain, shift, scale):
    return rms_norm(x, gain) * (1 + scale) + shift


def ada_terms(cond, w, b):
    m = jax.nn.silu(cond) @ w + b
    return jnp.split(m[..., None, :], N_MOD, axis=-1)


def squared_relu_mlp(h, w1, w2):
    return jnp.square(jax.nn.relu(h @ w1)) @ w2


def short_conv(u, w, b):
    L = u.shape[1]
    pad = HY_SHORT // 2
    up = jnp.pad(u, ((0, 0), (pad, pad), (0, 0)))
    out = b
    for j in range(HY_SHORT):
        out = out + up[:, j:j + L] * w[j]
    return out


def hyena_filter_spectrum(L, w1, b1, sin_freq, w2, b2, w3):
    f32 = jnp.float32
    t = jnp.linspace(0.0, 1.0, L, dtype=f32)[:, None]
    bands = jnp.linspace(1e-4, HY_BANDS - 1, HY_BANDS, dtype=f32)[None, :]
    ang = (2.0 * math.pi / L) * jnp.arange(L, dtype=f32)[:, None] * bands
    z = jnp.concatenate([t, jnp.cos(ang), -jnp.sin(ang)], axis=-1)
    h = jnp.sin(sin_freq[0] * (z @ w1 + b1))
    h = jnp.sin(sin_freq[1] * (h @ w2 + b2))
    h = (h @ w3).reshape(L, 2, HY_ORDER, HY_WIDTH)
    max_decay = math.log(HY_TARGET) / HY_FAST_PCT
    min_decay = math.log(HY_TARGET) / HY_SLOW_PCT
    deltas = jnp.abs(jnp.linspace(min_decay, max_decay, HY_WIDTH, dtype=f32))
    h = h * jnp.exp(-t[:, :, None, None] * deltas)
    fwd, bwd = h[:, 0], h[:, 1]
    kern = jnp.concatenate([(fwd[0] + bwd[0])[None], fwd[1:], jnp.zeros_like(fwd[:1]), bwd[1:][::-1]], axis=0)
    kern = kern / jnp.sum(jnp.abs(kern), axis=0, keepdims=True)
    return jnp.fft.rfft(kern, axis=0)


def long_conv(z, kf, bias):
    L = z.shape[1]
    zf = z.astype(jnp.float32)
    y = jnp.fft.irfft(jnp.fft.rfft(zf, n=2 * L, axis=1) * kf, n=2 * L, axis=1)[:, :L]
    return (y + zf * bias).astype(z.dtype)


def hyena_mixer(u, conv_w, conv_b, w1, b1, sin_freq, w2, b2, w3, bias):
    f32 = jnp.float32
    L = u.shape[1]
    u = short_conv(u, conv_w, conv_b)
    x1, x2, v = jnp.split(u, 3, axis=-1)
    kf = hyena_filter_spectrum(L, w1.astype(f32), b1.astype(f32), sin_freq.astype(f32),
                               w2.astype(f32), b2.astype(f32), w3.astype(f32))
    bias = bias.astype(f32)
    z = x1 * long_conv(v, kf[:, 0], bias[0])
    return x2 * long_conv(z, kf[:, 1], bias[1])


def retention_heads(cols):
    B, L, _ = cols.shape
    hk, hv = RET_HEADS * RET_DK, RET_HEADS * RET_DV
    q, k, v, g = jnp.split(cols, [hk, 2 * hk, 2 * hk + hv], axis=-1)
    to_heads = lambda a, d: a.reshape(B, L, RET_HEADS, d).transpose(0, 2, 1, 3)
    return to_heads(q, RET_DK), to_heads(k, RET_DK) * RET_DK ** -0.5, to_heads(v, RET_DV), g


def retention_final_state(k, v, log_gamma):
    L = k.shape[2]
    w = jnp.exp(log_gamma[:, None] * (L - 1 - jnp.arange(L, dtype=jnp.float32)))
    return jnp.einsum('bhld,bhle,hl->bhde', k.astype(jnp.float32), v.astype(jnp.float32), w)


def retention_chunkwise(q, k, v, log_gamma, s0):
    f32 = jnp.float32
    B, H, L, dk = q.shape
    dv = v.shape[-1]
    C = RET_CHUNK
    N = L // C
    qc = q.astype(f32).reshape(B, H, N, C, dk)
    kc = k.astype(f32).reshape(B, H, N, C, dk)
    vc = v.astype(f32).reshape(B, H, N, C, dv)
    pos = jnp.arange(C, dtype=f32)
    lg = log_gamma[:, None]
    diff = pos[:, None] - pos[None, :]
    decay = jnp.where(diff >= 0, jnp.exp(lg[:, :, None] * jnp.maximum(diff, 0.0)), 0.0)
    scores = jnp.einsum('bhncd,bhnmd->bhncm', qc, kc) * decay[:, None]
    inner = jnp.einsum('bhncm,bhnme->bhnce', scores, vc)
    zeta = jnp.exp(lg * (C - 1 - pos))
    xi = jnp.exp(lg * (pos + 1))
    chunk_kv = jnp.einsum('bhnmd,bhnme,hm->nbhde', kc, vc, zeta)
    g_chunk = jnp.exp(log_gamma * C)[:, None, None]

    def step(s, kv):
        return g_chunk * s + kv, s

    _, s_prev = lax.scan(step, s0.astype(f32), chunk_kv)
    cross = jnp.einsum('bhncd,nbhde,hc->bhnce', qc, s_prev, xi)
    return (inner + cross).reshape(B, H, L, dv)


def retention_bidir(q, k, v, lg_f, lg_b, s_f, s_b):
    flip = lambda a: jnp.flip(a, axis=2)
    out_f = retention_chunkwise(q, k, v, lg_f, s_f)
    out_b = flip(retention_chunkwise(flip(q), flip(k), flip(v), lg_b, s_b))
    return out_f + out_b


def retention_output(o, g):
    B, H, L, dv = o.shape
    o = o * lax.rsqrt(jnp.mean(o * o, axis=-1, keepdims=True) + EPS)
    o = o.transpose(0, 2, 1, 3).reshape(B, L, H * dv).astype(g.dtype)
    return jax.nn.silu(g) * o


def axial_rope_tables(L, dtype):
    f32 = jnp.float32
    rows = L // GRID_W
    row = jnp.repeat(jnp.arange(rows), GRID_W).astype(f32)
    col = jnp.tile(jnp.arange(GRID_W), rows).astype(f32)
    n_freq = MLA_ROPE // 4
    inv = ROPE_BASE ** (-jnp.arange(n_freq, dtype=f32) / n_freq)
    ang_r = row[:, None] * inv
    ang_c = col[:, None] * inv
    return tuple(a[:, None, :].astype(dtype) for a in
                 (jnp.cos(ang_r), jnp.sin(ang_r), jnp.cos(ang_c), jnp.sin(ang_c)))


def rotate(x, cos, sin):
    x1, x2 = jnp.split(x, 2, axis=-1)
    return jnp.concatenate([x1 * cos - x2 * sin, x1 * sin + x2 * cos], axis=-1)


def apply_axial_rope(x, cos_r, sin_r, cos_c, sin_c):
    xr, xc = jnp.split(x, 2, axis=-1)
    return jnp.concatenate([rotate(xr, cos_r, sin_r), rotate(xc, cos_c, sin_c)], axis=-1)


def mla_split(cols):
    return jnp.split(cols, [MLA_Q_LORA, MLA_Q_LORA + MLA_KV_LORA], axis=-1)


def mla_queries(c_q, q_norm_g, w_uq, rope):
    B, L, _ = c_q.shape
    q = (rms_norm(c_q, q_norm_g) @ w_uq).reshape(B, L, MLA_HEADS, MLA_QK)
    q_nope, q_rope = q[..., :MLA_NOPE], q[..., MLA_NOPE:]
    if rope is not None:
        q_rope = apply_axial_rope(q_rope, *rope)
    return jnp.concatenate([q_nope, q_rope], axis=-1) * MLA_QK ** -0.5


def mla_keys_values(c_kv, k_rope, kv_norm_g, w_ukv, rope):
    B, L, _ = c_kv.shape
    kv = (rms_norm(c_kv, kv_norm_g) @ w_ukv).reshape(B, L, MLA_HEADS, MLA_NOPE + MLA_V)
    k_nope, v = kv[..., :MLA_NOPE], kv[..., MLA_NOPE:]
    k_rope = k_rope[:, :, None, :]
    if rope is not None:
        k_rope = apply_axial_rope(k_rope, *rope)
    k = jnp.concatenate([k_nope, jnp.broadcast_to(k_rope, (B, L, MLA_HEADS, MLA_ROPE))], axis=-1)
    return k, v


def block_softmax_attention(q, k, v):
    B, Lq, H, dq = q.shape
    nb = Lq // ATTN_BLOCK
    qb = jnp.moveaxis(q.reshape(B, nb, ATTN_BLOCK, H, dq), 1, 0)

    def one_block(qi):
        s = jnp.einsum('bqhd,bkhd->bhqk', qi, k, preferred_element_type=jnp.float32)
        p = jax.nn.softmax(s, axis=-1).astype(v.dtype)
        return jnp.einsum('bhqk,bkhd->bqhd', p, v)

    o = lax.map(one_block, qb)
    return jnp.moveaxis(o, 0, 1).reshape(B, Lq, H * v.shape[-1])


def trunk_layer(x, xc, c, c_ctx, rope, p, last):
    f32 = jnp.float32
    sh1, sc1, ga1, sh2, sc2, ga2 = ada_terms(c, p['w_ada'], p['b_ada'])
    csh1, csc1, cga1, csh2, csc2, cga2 = ada_terms(c_ctx, p['w_ada'], p['b_ada'])

    u = modulate(x, p['norm1_g'], sh1, sc1) @ p['w_in']
    uc = modulate(xc, p['norm1_g'], csh1, csc1) @ p['w_in']
    u_hy, u_ret, u_mla = jnp.split(u, [HY_COLS, HY_COLS + RET_COLS], axis=-1)
    uc_hy, uc_ret, uc_mla = jnp.split(uc, [HY_COLS, HY_COLS + RET_COLS], axis=-1)
    hy_params = (p['hy_conv_w'], p['hy_conv_b'], p['hy_ffn_w1'], p['hy_ffn_b1'], p['hy_sin_freq'],
                 p['hy_ffn_w2'], p['hy_ffn_b2'], p['hy_ffn_w3'], p['hy_bias'])

    hy = hyena_mixer(u_hy, *hy_params)

    lg = jnp.log1p(-jnp.exp(p['ret_log_decay'].astype(f32)))
    q, k, v, g = retention_heads(u_ret)
    cq, ck, cv, cg = retention_heads(uc_ret)
    s_f = retention_final_state(ck, cv, lg[0])
    s_b = retention_final_state(jnp.flip(ck, axis=2), jnp.flip(cv, axis=2), lg[1])
    ret = retention_output(retention_bidir(q, k, v, lg[0], lg[1], s_f, s_b), g)

    lcq, lckv, lkr = mla_split(u_mla)
    ccq, cckv, ckr = mla_split(uc_mla)
    k_ctx, v_ctx = mla_keys_values(cckv, ckr, p['mla_kv_norm_g'], p['mla_w_ukv'], None)
    k_lat, v_lat = mla_keys_values(lckv, lkr, p['mla_kv_norm_g'], p['mla_w_ukv'], rope)
    q_lat = mla_queries(lcq, p['mla_q_norm_g'], p['mla_w_uq'], rope)
    att = block_softmax_attention(q_lat, jnp.concatenate([k_lat, k_ctx], axis=1),
                                  jnp.concatenate([v_lat, v_ctx], axis=1))

    x = x + ga1 * (jnp.concatenate([hy, ret, att], axis=-1) @ p['w_out'])
    x = x + ga2 * squared_relu_mlp(modulate(x, p['norm2_g'], sh2, sc2), p['mlp_w1'], p['mlp_w2'])

    if not last:
        hy_c = hyena_mixer(uc_hy, *hy_params)
        zero = jnp.zeros_like(s_f)
        ret_c = retention_output(retention_bidir(cq, ck, cv, lg[0], lg[1], zero, zero), cg)
        att_c = block_softmax_attention(mla_queries(ccq, p['mla_q_norm_g'], p['mla_w_uq'], None), k_ctx, v_ctx)
        xc = xc + cga1 * (jnp.concatenate([hy_c, ret_c, att_c], axis=-1) @ p['w_out'])
        xc = xc + cga2 * squared_relu_mlp(modulate(xc, p['norm2_g'], csh2, csc2), p['mlp_w1'], p['mlp_w2'])
    return x, xc


def setup_inputs(seed: int = 0) -> dict:
    key = jax.random.key(seed)
    ks = iter(jax.random.split(key, 32))
    nrm = lambda shape, scale: jax.random.normal(next(ks), shape, jnp.float32) * scale
    L = DEPTH
    ret_base = -(5.0 + jnp.arange(RET_HEADS, dtype=jnp.float32)) * math.log(2.0)
    return {
        'x': nrm((BATCH, SEQ, D_MODEL), 1.0),
        'c': nrm((BATCH, D_MODEL), 1.0),
        'ctx': nrm((BATCH, CTX_LEN, D_MODEL), 1.0),
        'c_ctx': nrm((D_MODEL,), 1.0),
        'w_ada': nrm((L, D_MODEL, N_MOD * D_MODEL), 0.5 * D_MODEL ** -0.5),
        'b_ada': nrm((L, N_MOD * D_MODEL), 0.02),
        'norm1_g': 1.0 + nrm((L, D_MODEL), 0.05),
        'norm2_g': 1.0 + nrm((L, D_MODEL), 0.05),
        'w_in': nrm((L, D_MODEL, IN_COLS), D_MODEL ** -0.5),
        'w_out': nrm((L, D_MIX, D_MODEL), D_MIX ** -0.5),
        'hy_conv_w': nrm((L, HY_SHORT, HY_COLS), HY_SHORT ** -0.5),
        'hy_conv_b': nrm((L, HY_COLS), 0.02),
        'hy_ffn_w1': nrm((L, HY_EMB, HY_FFN), HY_EMB ** -0.5),
        'hy_ffn_b1': nrm((L, HY_FFN), 0.1),
        'hy_sin_freq': 1.0 + nrm((L, 2, HY_FFN), 0.05),
        'hy_ffn_w2': nrm((L, HY_FFN, HY_FFN), HY_FFN ** -0.5),
        'hy_ffn_b2': nrm((L, HY_FFN), 0.1),
        'hy_ffn_w3': nrm((L, HY_FFN, HY_FILT_OUT), HY_FFN ** -0.5),
        'hy_bias': nrm((L, HY_ORDER, HY_WIDTH), 0.5),
        'ret_log_decay': ret_base + nrm((L, 2, RET_HEADS), 0.1),
        'mla_q_norm_g': 1.0 + nrm((L, MLA_Q_LORA), 0.05),
        'mla_w_uq': nrm((L, MLA_Q_LORA, MLA_HEADS * MLA_QK), MLA_Q_LORA ** -0.5),
        'mla_kv_norm_g': 1.0 + nrm((L, MLA_KV_LORA), 0.05),
        'mla_w_ukv': nrm((L, MLA_KV_LORA, MLA_HEADS * (MLA_NOPE + MLA_V)), MLA_KV_LORA ** -0.5),
        'mlp_w1': nrm((L, D_MODEL, D_FF), D_MODEL ** -0.5),
        'mlp_w2': nrm((L, D_FF, D_MODEL), D_FF ** -0.5),
        'final_norm_g': 1.0 + nrm((D_MODEL,), 0.05),
    }


def reference(x, c, ctx, c_ctx, w_ada, b_ada, norm1_g, norm2_g, w_in, w_out, hy_conv_w, hy_conv_b,
              hy_ffn_w1, hy_ffn_b1, hy_sin_freq, hy_ffn_w2, hy_ffn_b2, hy_ffn_w3, hy_bias, ret_log_decay,
              mla_q_norm_g, mla_w_uq, mla_kv_norm_g, mla_w_ukv, mlp_w1, mlp_w2, final_norm_g):
    rope = axial_rope_tables(x.shape[1], x.dtype)
    xc = ctx
    for i in range(DEPTH):
        p = {
            'w_ada': w_ada[i], 'b_ada': b_ada[i], 'norm1_g': norm1_g[i], 'norm2_g': norm2_g[i],
            'w_in': w_in[i], 'w_out': w_out[i], 'hy_conv_w': hy_conv_w[i], 'hy_conv_b': hy_conv_b[i],
            'hy_ffn_w1': hy_ffn_w1[i], 'hy_ffn_b1': hy_ffn_b1[i], 'hy_sin_freq': hy_sin_freq[i],
            'hy_ffn_w2': hy_ffn_w2[i], 'hy_ffn_b2': hy_ffn_b2[i], 'hy_ffn_w3': hy_ffn_w3[i],
            'hy_bias': hy_bias[i], 'ret_log_decay': ret_log_decay[i], 'mla_q_norm_g': mla_q_norm_g[i],
            'mla_w_uq': mla_w_uq[i], 'mla_kv_norm_g': mla_kv_norm_g[i], 'mla_w_ukv': mla_w_ukv[i],
            'mlp_w1': mlp_w1[i], 'mlp_w2': mlp_w2[i],
        }
        x, xc = trunk_layer(x, xc, c, c_ctx, rope, p, last=(i == DEPTH - 1))
    return rms_norm(x, final_norm_g)
```

```python
import functools
import math

import numpy as np
import jax
import jax.numpy as jnp
from jax import lax
from jax.experimental import pallas as pl
from jax.experimental.pallas import tpu as pltpu

F32 = jnp.float32
BF16 = jnp.bfloat16

D_MODEL = 1024
DEPTH = 2
GRID_W = 64
D_FF = 4 * D_MODEL
N_MOD = 6
EPS = 1e-6

HY_WIDTH = D_MODEL // 4
HY_ORDER = 2
HY_EMB = 33
HY_BANDS = (HY_EMB - 1) // 2
HY_FFN = 64
HY_TARGET = 1e-2
HY_FAST_PCT = 0.3
HY_SLOW_PCT = 1.5

RET_DK = 64
RET_DV = 64
RET_HEADS = (D_MODEL // 4) // RET_DV
RET_CHUNK = 128
RET_W = RET_HEADS * RET_DK

MLA_V = 64
MLA_HEADS = (D_MODEL // 2) // MLA_V
MLA_NOPE = 64
MLA_ROPE = 32
MLA_QK = MLA_NOPE + MLA_ROPE
MLA_Q_LORA = D_MODEL // 4
MLA_KV_LORA = D_MODEL // 8
ROPE_BASE = 10000.0

LANE = 128
HEAD_PAD = LANE
MLA_QW = MLA_HEADS * HEAD_PAD
MLA_VW = MLA_HEADS * MLA_V
MLA_IN_PAD = MLA_Q_LORA + MLA_KV_LORA + HEAD_PAD
IN_PAD = 3 * HY_WIDTH + 4 * RET_W + MLA_IN_PAD
DFT = 128
VMEM_LIMIT = 56 * 1024 * 1024


def _params(sem, vmem=None):
    return pltpu.CompilerParams(dimension_semantics=sem, vmem_limit_bytes=vmem)


def _full(shape):
    n = len(shape)
    return pl.BlockSpec(shape, lambda *_: (0,) * n, pipeline_mode=pl.Buffered(1))


def _dot(a, b):
    return jnp.dot(a, b, preferred_element_type=F32)


def _rms(x, g):
    return x * lax.rsqrt(jnp.mean(x * x, axis=-1, keepdims=True) + EPS) * g


def _ada_kernel(c_ref, w_ref, b_ref, o_ref):
    s = jax.nn.silu(c_ref[...])
    o_ref[...] = _dot(s.astype(BF16), w_ref[...].astype(BF16)) + b_ref[...]


def _ada(cond8, w_ada, b_ada):
    depth, d, n = w_ada.shape
    tn = 1536
    return pl.pallas_call(
        _ada_kernel,
        grid=(depth, n // tn),
        in_specs=[pl.BlockSpec((8, d), lambda l, j: (0, 0)),
                  pl.BlockSpec((None, d, tn), lambda l, j: (l, 0, j)),
                  pl.BlockSpec((None, 1, tn), lambda l, j: (l, 0, j))],
        out_specs=pl.BlockSpec((None, 8, tn), lambda l, j: (l, 0, j)),
        out_shape=jax.ShapeDtypeStruct((depth, 8, n), F32),
        compiler_params=_params(("arbitrary", "arbitrary"), VMEM_LIMIT),
        name="ada",
    )(cond8, w_ada, b_ada.reshape(depth, 1, n))


def _rope(t, c, s1, s2):
    return t * c + pltpu.roll(t, LANE - 8, 1) * s1 + pltpu.roll(t, 8, 1) * s2


def _inproj_kernel(x_ref, sh_ref, sc_ref, g_ref, win_ref, qg_ref, kvg_ref, wuq_ref, wuk_ref, wuv_ref,
                   rc_ref, rs1_ref, rs2_ref,
                   ux1_ref, ux2_ref, uv_ref, rq_ref, rk_ref, rv_ref, rg_ref, mq_ref, mk_ref, mv_ref):
    w = HY_WIDTH
    h = (_rms(x_ref[...], g_ref[...]) * (1 + sc_ref[...]) + sh_ref[...]).astype(BF16)
    ux1_ref[...] = _dot(h, win_ref[:, 0:w])
    ux2_ref[...] = _dot(h, win_ref[:, w:2 * w])
    uv_ref[...] = _dot(h, win_ref[:, 2 * w:3 * w])
    o = 3 * w
    rq_ref[...] = _dot(h, win_ref[:, o:o + RET_W])
    rk_ref[...] = _dot(h, win_ref[:, o + RET_W:o + 2 * RET_W]) * RET_DK ** -0.5
    rv_ref[...] = _dot(h, win_ref[:, o + 2 * RET_W:o + 3 * RET_W])
    rg_ref[...] = _dot(h, win_ref[:, o + 3 * RET_W:o + 4 * RET_W])
    o = o + 4 * RET_W
    um = _dot(h, win_ref[:, o:o + MLA_IN_PAD])
    cq = um[:, :MLA_Q_LORA]
    ckv = um[:, MLA_Q_LORA:MLA_Q_LORA + MLA_KV_LORA]
    kr = um[:, MLA_Q_LORA + MLA_KV_LORA:]
    rc, rs1, rs2 = rc_ref[...], rs1_ref[...], rs2_ref[...]
    q = _dot(_rms(cq, qg_ref[...]).astype(BF16), wuq_ref[...])
    for hd in range(MLA_HEADS):
        sl = slice(hd * HEAD_PAD, (hd + 1) * HEAD_PAD)
        mq_ref[:, sl] = (_rope(q[:, sl], rc, rs1, rs2) * MLA_QK ** -0.5).astype(BF16)
    ckvn = _rms(ckv, kvg_ref[...]).astype(BF16)
    kn = _dot(ckvn, wuk_ref[...])
    krr = _rope(kr, rc, rs1, rs2)
    for hd in range(MLA_HEADS):
        sl = slice(hd * HEAD_PAD, (hd + 1) * HEAD_PAD)
        mk_ref[:, sl] = (kn[:, sl] + krr).astype(BF16)
    mv_ref[...] = _dot(ckvn, wuv_ref[...]).astype(BF16)


def _inproj(x, sh, sc, g, lw, rope_tabs, tm):
    b, t, d = x.shape
    tok = lambda n: pl.BlockSpec((None, tm, n), lambda i, j: (i, j, 0))
    mod = pl.BlockSpec((None, 1, d), lambda i, j: (i, 0, 0))
    tab = pl.BlockSpec((tm, LANE), lambda i, j: (j, 0))
    outs = [(HY_WIDTH, F32)] * 3 + [(RET_W, F32)] * 4 + [(MLA_QW, BF16), (MLA_QW, BF16), (MLA_VW, BF16)]
    return pl.pallas_call(
        _inproj_kernel,
        grid=(b, t // tm),
        in_specs=[tok(d), mod, mod, _full((1, d)), _full(lw["w_in"].shape), _full((1, MLA_Q_LORA)),
                  _full((1, MLA_KV_LORA)), _full(lw["w_uq"].shape), _full(lw["w_uk"].shape),
                  _full(lw["w_uv"].shape), tab, tab, tab],
        out_specs=[tok(n) for n, _ in outs],
        out_shape=[jax.ShapeDtypeStruct((b, t, n), dt) for n, dt in outs],
        compiler_params=_params(("arbitrary", "arbitrary"), VMEM_LIMIT),
        name="inproj",
    )(x, sh, sc, g, lw["w_in"], lw["q_g"], lw["kv_g"], lw["w_uq"], lw["w_uk"], lw["w_uv"], *rope_tabs)


def _sconv_kernel(u_ref, w_ref, b_ref, o_ref, *, rows):
    t, cw = u_ref.shape
    w = w_ref[...]
    bias = b_ref[...]
    first = lax.broadcasted_iota(jnp.int32, (rows, cw), 0) == 0
    last = lax.broadcasted_iota(jnp.int32, (rows, cw), 0) == rows - 1
    zero = jnp.zeros((1, cw), F32)
    for r0 in range(0, t, rows):
        cur = u_ref[r0:r0 + rows, :]
        prev = u_ref[r0 - 1:r0, :] if r0 > 0 else zero
        nxt = u_ref[r0 + rows:r0 + rows + 1, :] if r0 + rows < t else zero
        up = jnp.where(first, prev, pltpu.roll(cur, 1, 0))
        dn = jnp.where(last, nxt, pltpu.roll(cur, rows - 1, 0))
        o_ref[r0:r0 + rows, :] = bias + up * w[0:1] + cur * w[1:2] + dn * w[2:3]


def _sconv(u, w, bias):
    b, t, c = u.shape
    cw = LANE
    return pl.pallas_call(
        functools.partial(_sconv_kernel, rows=min(t, 512)),
        grid=(b, c // cw),
        in_specs=[pl.BlockSpec((None, t, cw), lambda i, j: (i, 0, j)),
                  pl.BlockSpec((3, cw), lambda i, j: (0, j)),
                  pl.BlockSpec((1, cw), lambda i, j: (0, j))],
        out_specs=pl.BlockSpec((None, t, cw), lambda i, j: (i, 0, j)),
        out_shape=jax.ShapeDtypeStruct((b, t, c), F32),
        compiler_params=_params(("arbitrary", "arbitrary"), VMEM_LIMIT),
        name="sconv",
    )(u, w, bias)


def _filt_kernel(z_ref, w1_ref, b1_ref, sf_ref, w2_ref, b2_ref, w3_ref, dl_ref, k_ref, nrm_ref, *, seq, rows):
    i = pl.program_id(0)
    hp = lax.Precision.HIGHEST
    z = z_ref[...]
    h = jnp.sin(sf_ref[0:1, :] * (jnp.dot(z, w1_ref[...], precision=hp, preferred_element_type=F32) + b1_ref[...]))
    h = jnp.sin(sf_ref[1:2, :] * (jnp.dot(h, w2_ref[...], precision=hp, preferred_element_type=F32) + b2_ref[...]))
    h3 = jnp.dot(h, w3_ref[...], precision=hp, preferred_element_type=F32)
    half = HY_ORDER * HY_WIDTH
    win = jnp.exp(-z[:, 0:1] * dl_ref[...])
    fwd = h3[:, :half] * win
    bwd = h3[:, half:] * win
    row = i * rows + lax.broadcasted_iota(jnp.int32, (rows, half), 0)
    k = jnp.where(row == 0, fwd + bwd, jnp.where(row < seq, fwd, jnp.where(row == seq, 0.0, bwd)))
    k_ref[...] = k

    @pl.when(i == 0)
    def _():
        nrm_ref[...] = jnp.zeros_like(nrm_ref)

    nrm_ref[...] += jnp.sum(jnp.abs(k), axis=0, keepdims=True)


def _hyena_filter(seq, fw):
    t = jnp.linspace(0.0, 1.0, seq, dtype=F32)[:, None]
    bands = jnp.linspace(1e-4, HY_BANDS - 1, HY_BANDS, dtype=F32)[None, :]
    ang = (2.0 * math.pi / seq) * jnp.arange(seq, dtype=F32)[:, None] * bands
    z = jnp.concatenate([t, jnp.cos(ang), -jnp.sin(ang)], axis=-1)
    z2 = jnp.concatenate([z, jnp.zeros((1, HY_EMB), F32), z[1:][::-1]], axis=0)
    z2 = jnp.pad(z2, ((0, 0), (0, LANE - HY_EMB)))
    max_decay = math.log(HY_TARGET) / HY_FAST_PCT
    min_decay = math.log(HY_TARGET) / HY_SLOW_PCT
    deltas = jnp.abs(jnp.linspace(min_decay, max_decay, HY_WIDTH, dtype=F32))
    dl = jnp.tile(deltas, HY_ORDER)[None, :]
    half = HY_ORDER * HY_WIDTH
    rows = min(2 * seq, 1024)
    return pl.pallas_call(
        functools.partial(_filt_kernel, seq=seq, rows=rows),
        grid=(2 * seq // rows,),
        in_specs=[pl.BlockSpec((rows, LANE), lambda i: (i, 0)), _full((LANE, LANE)), _full((1, LANE)),
                  _full((2, LANE)), _full((LANE, LANE)), _full((1, LANE)), _full((LANE, 2 * half)),
                  _full((1, half))],
        out_specs=[pl.BlockSpec((rows, half), lambda i: (i, 0)), pl.BlockSpec((1, half), lambda i: (0, 0))],
        out_shape=[jax.ShapeDtypeStruct((2 * seq, half), F32), jax.ShapeDtypeStruct((1, half), F32)],
        compiler_params=_params(("arbitrary",), VMEM_LIMIT),
        name="hyena_filter",
    )(z2, fw["w1"], fw["b1"], fw["sf"], fw["w2"], fw["b2"], fw["w3"], dl)


def _dft_consts():
    idx = np.arange(DFT)
    ang = 2.0 * np.pi * ((idx[:, None] * idx[None, :]) % DFT) / DFT
    fcat = np.concatenate([np.cos(ang), -np.sin(ang)], axis=0)
    n = DFT * DFT
    tang = 2.0 * np.pi * (idx[:, None] * idx[None, :]) / n
    return (jnp.asarray(fcat, F32), jnp.asarray(np.cos(tang), F32), jnp.asarray(-np.sin(tang), F32))


def _cmul(ar, ai, br, bi):
    return ar * br - ai * bi, ar * bi + ai * br


def _fft1_kernel(*refs, n_b, cw, has_im):
    if has_im:
        x_ref, f_ref, twr_ref, twi_ref, or_ref, oi_ref = refs
    else:
        x_ref, f_ref, twr_ref, twi_ref, or_ref, oi_ref = refs
    f = f_ref[...]
    for j in range(n_b):
        sl = slice(j * cw, (j + 1) * cw)
        if has_im:
            xx = jnp.concatenate([x_ref[0, :, sl], x_ref[1, :, sl]], axis=1).astype(BF16)
            o = _dot(f, xx)
            ar = o[:DFT, :cw] - o[DFT:, cw:]
            ai = o[:DFT, cw:] + o[DFT:, :cw]
        else:
            o = _dot(f, x_ref[:, sl].astype(BF16))
            ar, ai = o[:DFT], o[DFT:]
        tr, ti = twr_ref[j], twi_ref[j]
        for s in range(cw // LANE):
            ls = slice(s * LANE, (s + 1) * LANE)
            pr, pi = _cmul(ar[:, ls], ai[:, ls], tr, ti)
            or_ref[:, j * cw + s * LANE:j * cw + (s + 1) * LANE] = pr.astype(BF16)
            oi_ref[:, j * cw + s * LANE:j * cw + (s + 1) * LANE] = pi.astype(BF16)


def _fft1(x, consts, n_a, cw, has_im, n_b=8):
    fcat, twr, twi = consts
    g = x.shape[0]
    f = fcat[:, :n_a].astype(BF16)
    if has_im:
        xspec = pl.BlockSpec((None, 2, n_a, n_b * cw), lambda j, i: (i, 0, 0, j))
    else:
        xspec = pl.BlockSpec((None, n_a, n_b * cw), lambda j, i: (i, 0, j))
    tw = pl.BlockSpec((n_b, DFT, LANE), lambda j, i: (j, 0, 0))
    ospec = pl.BlockSpec((None, DFT, n_b * cw), lambda j, i: (i, 0, j))
    oshape = jax.ShapeDtypeStruct((g, DFT, DFT * cw), BF16)
    return pl.pallas_call(
        functools.partial(_fft1_kernel, n_b=n_b, cw=cw, has_im=has_im),
        grid=(DFT // n_b, g),
        in_specs=[xspec, _full(f.shape), tw, tw],
        out_specs=[ospec, ospec],
        out_shape=[oshape, oshape],
        compiler_params=_params(("arbitrary", "arbitrary"), VMEM_LIMIT),
        name="fft_stage1",
    )(x, f, twr, twi)


def _fft2_filter_kernel(ar_ref, ai_ref, f_ref, nrm_ref, kr_ref, ki_ref, *, qb, cw):
    f = f_ref[...]
    scale = 1.0 / (nrm_ref[...] * float(DFT * DFT))
    for qi in range(qb):
        o = _dot(f, jnp.concatenate([ar_ref[qi], ai_ref[qi]], axis=1))
        kr_ref[qi] = (o[:DFT, :cw] - o[DFT:, cw:]) * scale
        ki_ref[qi] = (o[:DFT, cw:] + o[DFT:, :cw]) * scale


def _fft2_filter(ar, ai, consts, nrm, cw, qb=4):
    fcat = consts[0].astype(BF16)
    spec = pl.BlockSpec((qb, DFT, cw), lambda i: (i, 0, 0))
    oshape = jax.ShapeDtypeStruct((DFT, DFT, cw), F32)
    return pl.pallas_call(
        functools.partial(_fft2_filter_kernel, qb=qb, cw=cw),
        grid=(DFT // qb,),
        in_specs=[spec, spec, _full(fcat.shape), _full((1, cw))],
        out_specs=[spec, spec],
        out_shape=[oshape, oshape],
        compiler_params=_params(("arbitrary",), VMEM_LIMIT),
        name="fft_filter_stage2",
    )(ar.reshape(DFT, DFT, cw), ai.reshape(DFT, DFT, cw), fcat, nrm)


def _fft2_kernel(ar_ref, ai_ref, f_ref, kr_ref, ki_ref, twr_ref, twi_ref, br_ref, bi_ref, *, qb, groups, cw):
    f = f_ref[...]
    for qi in range(qb):
        parts = []
        for g in range(groups):
            parts += [ar_ref[g, qi], ai_ref[g, qi]]
        o = _dot(f, jnp.concatenate(parts, axis=1))
        kr, ki = kr_ref[qi], ki_ref[qi]
        ys = []
        for g in range(groups):
            rr = slice(2 * g * cw, (2 * g + 1) * cw)
            ii = slice((2 * g + 1) * cw, (2 * g + 2) * cw)
            xr = o[:DFT, rr] - o[DFT:, ii]
            xi = o[:DFT, ii] + o[DFT:, rr]
            yr, yi = _cmul(xr, xi, kr, ki)
            ys += [yr.astype(BF16), yi.astype(BF16)]
        o2 = _dot(f, jnp.concatenate(ys, axis=1))
        tr, ti = twr_ref[qi], twi_ref[qi]
        for g in range(groups):
            rr = slice(2 * g * cw, (2 * g + 1) * cw)
            ii = slice((2 * g + 1) * cw, (2 * g + 2) * cw)
            br = o2[:DFT, rr] + o2[DFT:, ii]
            bi = o2[:DFT, ii] - o2[DFT:, rr]
            for s in range(cw // LANE):
                ls = slice(s * LANE, (s + 1) * LANE)
                pr, pi = _cmul(br[:, ls], bi[:, ls], tr, -ti)
                br_ref[g, qi, :, ls] = pr.astype(BF16)
                bi_ref[g, qi, :, ls] = pi.astype(BF16)


def _fft2(ar, ai, consts, kr, ki, order, cw, qb=4):
    fcat, twr, twi = consts
    fcat = fcat.astype(BF16)
    g = ar.shape[0]
    a4 = pl.BlockSpec((g, qb, DFT, cw), lambda i: (0, i, 0, 0))
    ksp = pl.BlockSpec((qb, DFT, cw), lambda i: (i, 0, order))
    tw = pl.BlockSpec((qb, DFT, LANE), lambda i: (i, 0, 0))
    oshape = jax.ShapeDtypeStruct((g, DFT, DFT, cw), BF16)
    return pl.pallas_call(
        functools.partial(_fft2_kernel, qb=qb, groups=g, cw=cw),
        grid=(DFT // qb,),
        in_specs=[a4, a4, _full(fcat.shape), ksp, ksp, tw, tw],
        out_specs=[a4, a4],
        out_shape=[oshape, oshape],
        compiler_params=_params(("arbitrary",), VMEM_LIMIT),
        name="fft_stage2",
    )(ar.reshape(g, DFT, DFT, cw), ai.reshape(g, DFT, DFT, cw), fcat, kr, ki, twr, twi)


def _fft3_kernel(br_ref, bi_ref, f_ref, z_ref, gate_ref, bias_ref, o_ref, *, n_b, cw, n_a):
    f = f_ref[...]
    bias = bias_ref[...]
    for j in range(n_b):
        sl = slice(j * cw, (j + 1) * cw)
        o = _dot(f, jnp.concatenate([br_ref[:, sl], bi_ref[:, sl]], axis=1))
        y0 = o[:n_a, :cw] + o[n_a:, cw:]
        y1 = o[:n_a, cw:] - o[n_a:, :cw]
        o_ref[0, :, sl] = gate_ref[0, :, sl] * (y0 + z_ref[0, :, sl] * bias)
        o_ref[1, :, sl] = gate_ref[1, :, sl] * (y1 + z_ref[1, :, sl] * bias)


def _fft3(br, bi, consts, z, gate, bias, n_a, cw, n_b=8):
    fcat = consts[0]
    f = jnp.concatenate([fcat[:n_a], fcat[DFT:DFT + n_a]], axis=0).astype(BF16)
    g = br.shape[0]
    bsp = pl.BlockSpec((None, DFT, n_b * cw), lambda j, i: (i, 0, j))
    zsp = pl.BlockSpec((None, 2, n_a, n_b * cw), lambda j, i: (i, 0, 0, j))
    return pl.pallas_call(
        functools.partial(_fft3_kernel, n_b=n_b, cw=cw, n_a=n_a),
        grid=(DFT // n_b, g),
        in_specs=[bsp, bsp, _full(f.shape), zsp, zsp, _full((1, cw))],
        out_specs=zsp,
        out_shape=jax.ShapeDtypeStruct(z.shape, F32),
        compiler_params=_params(("arbitrary", "arbitrary"), VMEM_LIMIT),
        name="fft_stage3",
    )(br.reshape(g, DFT, DFT * cw), bi.reshape(g, DFT, DFT * cw), f, z, gate, bias)


def _hyena_long(x1, x2, v, fw, bias):
    b, seq, w = v.shape
    assert 2 * seq == DFT * DFT and b % 2 == 0
    n_a = seq // DFT
    consts = _dft_consts()
    fcat, twr, twi = consts
    consts = (fcat, jnp.broadcast_to(twr[:, :, None], (DFT, DFT, LANE)),
              jnp.broadcast_to(twi[:, :, None], (DFT, DFT, LANE)))
    kf, nrm = _hyena_filter(seq, fw)
    half = HY_ORDER * w
    far, fai = _fft1(kf.reshape(1, DFT, DFT * half), consts, DFT, half, False, n_b=4)
    kr, ki = _fft2_filter(far, fai, consts, nrm, half)
    view = lambda a: a.reshape(b // 2, 2, n_a, DFT * w)
    z, gates = view(v), (view(x1), view(x2))
    for o in range(HY_ORDER):
        ar, ai = _fft1(z, consts, n_a, w, True)
        br, bi = _fft2(ar, ai, consts, kr, ki, o, w)
        z = _fft3(br, bi, consts, z, gates[o], bias[o:o + 1], n_a, w)
    return z.reshape(b, seq, w)


def _ctx_hyena_kernel(x1_ref, x2_ref, v_ref, kf_ref, nrm_ref, fd_ref, fi_ref, bias_ref, o_ref, *, seq, w):
    n = 2 * seq
    fd = fd_ref[...]
    kc = _dot(fd, kf_ref[...].astype(BF16))
    scale = 1.0 / (nrm_ref[...] * float(n))
    kr, ki = kc[:n] * scale, kc[n:] * scale
    finv = fi_ref[...]
    z = v_ref[...]
    gates = (x1_ref[...], x2_ref[...])
    for o in range(HY_ORDER):
        sl = slice(o * w, (o + 1) * w)
        xf = _dot(fd[:, :seq], z.astype(BF16))
        yr, yi = _cmul(xf[:n], xf[n:], kr[:, sl], ki[:, sl])
        y = _dot(finv, jnp.concatenate([yr, yi], axis=0).astype(BF16))
        z = gates[o] * (y + z * bias_ref[o:o + 1, :])
    o_ref[...] = z


def _hyena_ctx(x1, x2, v, fw, bias):
    b, seq, w = v.shape
    n = 2 * seq
    idx = np.arange(n)
    ang = 2.0 * np.pi * ((idx[:, None] * idx[None, :]) % n) / n
    fr, fi = np.cos(ang), -np.sin(ang)
    fd = jnp.asarray(np.concatenate([fr, fi], axis=0), F32).astype(BF16)
    finv = jnp.asarray(np.concatenate([fr[:seq], fi[:seq]], axis=1), F32).astype(BF16)
    kf, nrm = _hyena_filter(seq, fw)
    tok = pl.BlockSpec((None, seq, w), lambda i: (i, 0, 0))
    return pl.pallas_call(
        functools.partial(_ctx_hyena_kernel, seq=seq, w=w),
        grid=(b,),
        in_specs=[tok, tok, tok, _full(kf.shape), _full(nrm.shape), _full(fd.shape), _full(finv.shape),
                  _full(bias.shape)],
        out_specs=tok,
        out_shape=jax.ShapeDtypeStruct(v.shape, F32),
        compiler_params=_params(("arbitrary",), VMEM_LIMIT),
        name="hyena_ctx",
    )(x1, x2, v, kf, nrm, fd, finv, bias)


def _ret_kernel(*refs, reverse, finalize, n_chunk):
    if finalize:
        q_ref, k_ref, v_ref, ld_ref, s0_ref, of_ref, g_ref, out_ref, sfin_ref, s_scr = refs
    else:
        q_ref, k_ref, v_ref, ld_ref, s0_ref, out_ref, sfin_ref, s_scr = refs
    c = RET_CHUNK
    t = pl.program_id(1)

    @pl.when(t == 0)
    def _():
        s_scr[...] = s0_ref[...]

    r = lax.broadcasted_iota(jnp.int32, (c, c), 0)
    m = lax.broadcasted_iota(jnp.int32, (c, c), 1)
    diff = ((m - r) if reverse else (r - m)).astype(F32)
    pos = lax.broadcasted_iota(jnp.int32, (c, RET_DV), 0).astype(F32)
    for hd in range(RET_HEADS):
        lg = jnp.log1p(-jnp.exp(ld_ref[hd]))
        lgv = lg[:, :RET_DV]
        decay = jnp.where(diff >= 0, jnp.exp(lg * jnp.maximum(diff, 0.0)), 0.0)
        if reverse:
            zeta = jnp.exp(lgv * pos)
            xi = jnp.exp(lgv * (c - pos))
        else:
            zeta = jnp.exp(lgv * (c - 1 - pos))
            xi = jnp.exp(lgv * (pos + 1))
        g_chunk = jnp.exp(lgv * c)
        hs = slice(hd * RET_DK, (hd + 1) * RET_DK)
        for ci in (range(n_chunk - 1, -1, -1) if reverse else range(n_chunk)):
            rows = slice(ci * c, (ci + 1) * c)
            qh = q_ref[rows, hs].astype(BF16)
            kh = k_ref[rows, hs].astype(BF16)
            vh = v_ref[rows, hs]
            s = lax.dot_general(qh, kh, (((1,), (1,)), ((), ())), preferred_element_type=F32)
            inner = _dot((s * decay).astype(BF16), vh.astype(BF16))
            state = s_scr[hd]
            o = inner + _dot(qh, state.astype(BF16)) * xi
            kv = lax.dot_general(kh, (vh * zeta).astype(BF16), (((0,), (0,)), ((), ())),
                                 preferred_element_type=F32)
            s_scr[hd] = g_chunk * state + kv
            if finalize:
                o = o + of_ref[rows, hs]
                o = o * lax.rsqrt(jnp.mean(o * o, axis=-1, keepdims=True) + EPS)
                o = jax.nn.silu(g_ref[rows, hs]) * o
            out_ref[rows, hs] = o

    @pl.when(t == pl.num_programs(1) - 1)
    def _():
        sfin_ref[...] = s_scr[...]


def _ret_sweep(q, k, v, ld, s0, reverse, fwd_out=None, gate=None):
    b, t, w = q.shape
    tt = min(t, 1024)
    nt = t // tt
    finalize = fwd_out is not None
    tmap = (lambda i, j: (i, nt - 1 - j, 0)) if reverse else (lambda i, j: (i, j, 0))
    tok = pl.BlockSpec((None, tt, w), tmap)
    st = pl.BlockSpec((None, RET_HEADS, RET_DK, RET_DV), lambda i, j: (i, 0, 0, 0))
    ins = [q, k, v, ld, s0] + ([fwd_out, gate] if finalize else [])
    in_specs = [tok, tok, tok, _full(ld.shape), st] + ([tok, tok] if finalize else [])
    return pl.pallas_call(
        functools.partial(_ret_kernel, reverse=reverse, finalize=finalize, n_chunk=tt // RET_CHUNK),
        grid=(b, nt),
        in_specs=in_specs,
        out_specs=[tok, st],
        out_shape=[jax.ShapeDtypeStruct((b, t, w), F32),
                   jax.ShapeDtypeStruct((b, RET_HEADS, RET_DK, RET_DV), F32)],
        scratch_shapes=[pltpu.VMEM((RET_HEADS, RET_DK, RET_DV), F32)],
        compiler_params=_params(("arbitrary", "arbitrary"), VMEM_LIMIT),
        name="retention_bwd" if reverse else "retention_fwd",
    )(*ins)


def _retention(q, k, v, g, ld, s_f, s_b):
    out_f, fin_f = _ret_sweep(q, k, v, ld[0], s_f, False)
    out, fin_b = _ret_sweep(q, k, v, ld[1], s_b, True, out_f, g)
    return out, fin_f, fin_b


def _attn_kernel(*refs, n_src, tk):
    q_ref = refs[0]
    kv = refs[1:1 + 2 * n_src]
    o_ref = refs[1 + 2 * n_src]
    tq = q_ref.shape[0]
    for hh in range(2):
        q = q_ref[:, hh * HEAD_PAD:(hh + 1) * HEAD_PAD]
        ks = slice(hh * HEAD_PAD, (hh + 1) * HEAD_PAD)
        vs = slice(hh * MLA_V, (hh + 1) * MLA_V)

        def step(kc, vc, carry):
            m, l, acc = carry
            s = lax.dot_general(q, kc, (((1,), (1,)), ((), ())), preferred_element_type=F32)
            mn = jnp.maximum(m, jnp.max(s, axis=-1, keepdims=True))
            a = jnp.exp(m - mn)
            p = jnp.exp(s - mn)
            l = a * l + jnp.sum(p, axis=-1, keepdims=True)
            acc = a * acc + _dot(p.astype(BF16), vc)
            return mn, l, acc

        carry = (jnp.full((tq, 1), -jnp.inf, F32), jnp.zeros((tq, 1), F32), jnp.zeros((tq, MLA_V), F32))
        for si in range(n_src):
            k_ref, v_ref = kv[2 * si], kv[2 * si + 1]
            n_k = k_ref.shape[0]
            blk = min(tk, n_k)

            def body(i, carry, k_ref=k_ref, v_ref=v_ref, blk=blk):
                off = pl.multiple_of(i * blk, blk)
                return step(k_ref[pl.ds(off, blk), ks], v_ref[pl.ds(off, blk), vs], carry)

            carry = lax.fori_loop(0, n_k // blk, body, carry)
        m, l, acc = carry
        o_ref[:, vs] = (acc / l).astype(o_ref.dtype)


def _attention(q, kvs, tq, tk=512):
    b, t, _ = q.shape
    in_specs = [pl.BlockSpec((None, tq, 2 * HEAD_PAD), lambda i, h, j: (i, j, h))]
    args = [q]
    for k, v in kvs:
        in_specs += [pl.BlockSpec((None, k.shape[1], 2 * HEAD_PAD), lambda i, h, j: (i, 0, h)),
                     pl.BlockSpec((None, v.shape[1], 2 * MLA_V), lambda i, h, j: (i, 0, h))]
        args += [k, v]
    return pl.pallas_call(
        functools.partial(_attn_kernel, n_src=len(kvs), tk=tk),
        grid=(b, MLA_HEADS // 2, t // tq),
        in_specs=in_specs,
        out_specs=pl.BlockSpec((None, tq, 2 * MLA_V), lambda i, h, j: (i, j, h)),
        out_shape=jax.ShapeDtypeStruct((b, t, MLA_VW), BF16),
        compiler_params=_params(("arbitrary", "arbitrary", "arbitrary"), VMEM_LIMIT),
        name="mla_attention",
    )(*args)


def _outmlp_kernel(*refs, final, ffc):
    if final:
        (x_ref, hy_ref, ret_ref, att_ref, ga1_ref, sh2_ref, sc2_ref, ga2_ref, g2_ref, wo_ref, w1_ref, w2_ref,
         fg_ref, o_ref) = refs
    else:
        (x_ref, hy_ref, ret_ref, att_ref, ga1_ref, sh2_ref, sc2_ref, ga2_ref, g2_ref, wo_ref, w1_ref, w2_ref,
         o_ref) = refs
    w = HY_WIDTH
    mix = (_dot(hy_ref[...].astype(BF16), wo_ref[0:w, :]) + _dot(ret_ref[...].astype(BF16), wo_ref[w:2 * w, :])
           + _dot(att_ref[...], wo_ref[2 * w:, :]))
    x = x_ref[...] + ga1_ref[...] * mix
    h = (_rms(x, g2_ref[...]) * (1 + sc2_ref[...]) + sh2_ref[...]).astype(BF16)
    acc = jnp.zeros(x.shape, F32)
    for c0 in range(0, D_FF, ffc):
        hid = jnp.square(jax.nn.relu(_dot(h, w1_ref[:, c0:c0 + ffc]))).astype(BF16)
        acc = acc + _dot(hid, w2_ref[c0:c0 + ffc, :])
    x = x + ga2_ref[...] * acc
    if final:
        x = _rms(x, fg_ref[...])
    o_ref[...] = x


def _outmlp(x, hy, ret, att, ga1, sh2, sc2, ga2, lw, final_g, tm):
    b, t, d = x.shape
    tok = lambda n: pl.BlockSpec((None, tm, n), lambda i, j: (i, j, 0))
    mod = pl.BlockSpec((None, 1, d), lambda i, j: (i, 0, 0))
    final = final_g is not None
    ins = [x, hy, ret, att, ga1, sh2, sc2, ga2, lw["norm2_g"], lw["w_out"], lw["w1"], lw["w2"]]
    in_specs = [tok(d), tok(hy.shape[-1]), tok(ret.shape[-1]), tok(att.shape[-1]), mod, mod, mod, mod,
                _full((1, d)), _full(lw["w_out"].shape), _full(lw["w1"].shape), _full(lw["w2"].shape)]
    if final:
        ins.append(final_g)
        in_specs.append(_full((1, d)))
    return pl.pallas_call(
        functools.partial(_outmlp_kernel, final=final, ffc=1024),
        grid=(b, t // tm),
        in_specs=in_specs,
        out_specs=tok(d),
        out_shape=jax.ShapeDtypeStruct((b, t, d), F32),
        compiler_params=_params(("arbitrary", "arbitrary"), VMEM_LIMIT),
        name="outproj_mlp",
    )(*ins)


def _rope_tables(t):
    rows = t // GRID_W
    row = jnp.repeat(jnp.arange(rows), GRID_W).astype(F32)
    col = jnp.tile(jnp.arange(GRID_W), rows).astype(F32)
    n_freq = MLA_ROPE // 4
    inv = ROPE_BASE ** (-jnp.arange(n_freq, dtype=F32) / n_freq)
    ang_r = row[:, None] * inv
    ang_c = col[:, None] * inv
    cr, sr, cc, sc = jnp.cos(ang_r), jnp.sin(ang_r), jnp.cos(ang_c), jnp.sin(ang_c)
    one = jnp.ones((t, MLA_NOPE), F32)
    z64 = jnp.zeros((t, MLA_NOPE), F32)
    z8 = jnp.zeros((t, n_freq), F32)
    tail1 = jnp.ones((t, HEAD_PAD - MLA_QK), F32)
    tail0 = jnp.zeros((t, HEAD_PAD - MLA_QK), F32)
    c = jnp.concatenate([one, cr, cr, cc, cc, tail1], axis=1)
    s1 = jnp.concatenate([z64, -sr, z8, -sc, z8, tail0], axis=1)
    s2 = jnp.concatenate([z64, z8, sr, z8, sc, tail0], axis=1)
    return c, s1, s2


def _identity_rope_tables(t):
    return jnp.ones((t, HEAD_PAD), F32), jnp.zeros((t, HEAD_PAD), F32), jnp.zeros((t, HEAD_PAD), F32)


def _layer_weights(p):
    d = D_MODEL
    w_in = p["w_in"]
    hy_cols = 3 * HY_WIDTH
    ret_cols = 4 * RET_W
    w_mla = w_in[:, hy_cols + ret_cols:]
    pad_l = jnp.zeros((d, MLA_NOPE), F32)
    pad_r = jnp.zeros((d, HEAD_PAD - MLA_QK), F32)
    w_mla = jnp.concatenate([w_mla[:, :MLA_Q_LORA + MLA_KV_LORA], pad_l, w_mla[:, MLA_Q_LORA + MLA_KV_LORA:], pad_r],
                            axis=1)
    w_in_pad = jnp.concatenate([w_in[:, :hy_cols + ret_cols], w_mla], axis=1).astype(BF16)
    w_uq = p["mla_w_uq"].reshape(MLA_Q_LORA, MLA_HEADS, MLA_QK)
    w_uq = jnp.pad(w_uq, ((0, 0), (0, 0), (0, HEAD_PAD - MLA_QK))).reshape(MLA_Q_LORA, MLA_QW).astype(BF16)
    w_ukv = p["mla_w_ukv"].reshape(MLA_KV_LORA, MLA_HEADS, MLA_NOPE + MLA_V)
    w_uk = jnp.pad(w_ukv[:, :, :MLA_NOPE], ((0, 0), (0, 0), (0, HEAD_PAD - MLA_NOPE)))
    w_uk = w_uk.reshape(MLA_KV_LORA, MLA_QW).astype(BF16)
    w_uv = w_ukv[:, :, MLA_NOPE:].reshape(MLA_KV_LORA, MLA_VW).astype(BF16)
    pad_f = LANE - HY_FFN
    fw = {
        "w1": jnp.pad(p["hy_ffn_w1"], ((0, LANE - HY_EMB), (0, pad_f))),
        "b1": jnp.pad(p["hy_ffn_b1"], (0, pad_f))[None, :],
        "sf": jnp.pad(p["hy_sin_freq"], ((0, 0), (0, pad_f))),
        "w2": jnp.pad(p["hy_ffn_w2"], ((0, pad_f), (0, pad_f))),
        "b2": jnp.pad(p["hy_ffn_b2"], (0, pad_f))[None, :],
        "w3": jnp.pad(p["hy_ffn_w3"], ((0, pad_f), (0, 0))),
    }
    ld = jnp.broadcast_to(p["ret_log_decay"][:, :, None, None], (2, RET_HEADS, 1, LANE))
    return {
        "w_in": w_in_pad, "w_uq": w_uq, "w_uk": w_uk, "w_uv": w_uv,
        "q_g": p["mla_q_norm_g"][None, :], "kv_g": p["mla_kv_norm_g"][None, :],
        "norm1_g": p["norm1_g"][None, :], "norm2_g": p["norm2_g"][None, :],
        "w_out": p["w_out"].astype(BF16), "w1": p["mlp_w1"].astype(BF16), "w2": p["mlp_w2"].astype(BF16),
        "conv_w": p["hy_conv_w"], "conv_b": p["hy_conv_b"][None, :], "hy_bias": p["hy_bias"],
        "filt": fw, "ld": ld,
    }


def _mixers(x, sh1, sc1, lw, rope_tabs, tm):
    ux1, ux2, uv, rq, rk, rv, rg, mq, mk, mv = _inproj(x, sh1, sc1, lw["norm1_g"], lw, rope_tabs, tm)
    w = HY_WIDTH
    cw, cb = lw["conv_w"], lw["conv_b"]
    x1 = _sconv(ux1, cw[:, 0:w], cb[:, 0:w])
    x2 = _sconv(ux2, cw[:, w:2 * w], cb[:, w:2 * w])
    v = _sconv(uv, cw[:, 2 * w:], cb[:, 2 * w:])
    return (x1, x2, v), (rq, rk, rv, rg), (mq, mk, mv)


def kernel(x, c, ctx, c_ctx, w_ada, b_ada, norm1_g, norm2_g, w_in, w_out, hy_conv_w, hy_conv_b, hy_ffn_w1,
           hy_ffn_b1, hy_sin_freq, hy_ffn_w2, hy_ffn_b2, hy_ffn_w3, hy_bias, ret_log_decay, mla_q_norm_g, mla_w_uq,
           mla_kv_norm_g, mla_w_ukv, mlp_w1, mlp_w2, final_norm_g):
    b, seq, d = x.shape
    n_ctx = ctx.shape[1]
    depth = w_ada.shape[0]
    cond8 = jnp.concatenate([c, c_ctx[None, :], jnp.zeros((8 - b - 1, d), F32)], axis=0)
    ada = _ada(cond8, w_ada, b_ada)
    rope_lat = _rope_tables(seq)
    rope_ctx = _identity_rope_tables(n_ctx)
    zero_state = jnp.zeros((b, RET_HEADS, RET_DK, RET_DV), F32)
    xc = ctx
    for i in range(depth):
        p = {
            "w_in": w_in[i], "w_out": w_out[i], "norm1_g": norm1_g[i], "norm2_g": norm2_g[i],
            "hy_conv_w": hy_conv_w[i], "hy_conv_b": hy_conv_b[i], "hy_ffn_w1": hy_ffn_w1[i],
            "hy_ffn_b1": hy_ffn_b1[i], "hy_sin_freq": hy_sin_freq[i], "hy_ffn_w2": hy_ffn_w2[i],
            "hy_ffn_b2": hy_ffn_b2[i], "hy_ffn_w3": hy_ffn_w3[i], "hy_bias": hy_bias[i],
            "ret_log_decay": ret_log_decay[i], "mla_q_norm_g": mla_q_norm_g[i], "mla_w_uq": mla_w_uq[i],
            "mla_kv_norm_g": mla_kv_norm_g[i], "mla_w_ukv": mla_w_ukv[i], "mlp_w1": mlp_w1[i], "mlp_w2": mlp_w2[i],
        }
        lw = _layer_weights(p)
        last = i == depth - 1
        terms = [ada[i, :, k * d:(k + 1) * d] for k in range(N_MOD)]
        lat = [tm[:b, None, :] for tm in terms]
        cx = [jnp.broadcast_to(tm[b:b + 1, None, :], (b, 1, d)) for tm in terms]

        hy_c, ret_c, mla_c = _mixers(xc, cx[0], cx[1], lw, rope_ctx, n_ctx)
        hy_l, ret_l, mla_l = _mixers(x, lat[0], lat[1], lw, rope_lat, 512)

        rq, rk, rv, rg = ret_c
        ret_ctx, s_f, s_b = _retention(rq, rk, rv, rg, lw["ld"], zero_state, zero_state)
        rq, rk, rv, rg = ret_l
        ret, _, _ = _retention(rq, rk, rv, rg, lw["ld"], s_f, s_b)

        hy = _hyena_long(*hy_l, lw["filt"], lw["hy_bias"])

        mq, mk, mv = mla_l
        cq, ck, cv = mla_c
        att = _attention(mq, [(mk, mv), (ck, cv)], 256)

        fg = final_norm_g[None, :] if last else None
        x = _outmlp(x, hy, ret, att, lat[2], lat[3], lat[4], lat[5], lw, fg, 512)
        if not last:
            hyc = _hyena_ctx(*hy_c, lw["filt"], lw["hy_bias"])
            att_c = _attention(cq, [(ck, cv)], n_ctx)
            xc = _outmlp(xc, hyc, ret_ctx, att_c, cx[2], cx[3], cx[4], cx[5], lw, None, n_ctx)
    return x
```

```python
import functools
import math

import numpy as np
import jax
import jax.numpy as jnp
from jax import lax
from jax.experimental import pallas as pl
from jax.experimental.pallas import tpu as pltpu

F32 = jnp.float32
BF16 = jnp.bfloat16

D_MODEL = 1024
DEPTH = 2
GRID_W = 64
D_FF = 4 * D_MODEL
N_MOD = 6
EPS = 1e-6

HY_WIDTH = D_MODEL // 4
HY_ORDER = 2
HY_EMB = 33
HY_BANDS = (HY_EMB - 1) // 2
HY_FFN = 64
HY_TARGET = 1e-2
HY_FAST_PCT = 0.3
HY_SLOW_PCT = 1.5

RET_DK = 64
RET_DV = 64
RET_HEADS = (D_MODEL // 4) // RET_DV
RET_KCHUNK = 256
RET_W = RET_HEADS * RET_DK

MLA_V = 64
MLA_HEADS = (D_MODEL // 2) // MLA_V
MLA_NOPE = 64
MLA_ROPE = 32
MLA_QK = MLA_NOPE + MLA_ROPE
MLA_Q_LORA = D_MODEL // 4
MLA_KV_LORA = D_MODEL // 8
ROPE_BASE = 10000.0

LANE = 128
HEAD_PAD = LANE
MLA_QW = MLA_HEADS * HEAD_PAD
MLA_VW = MLA_HEADS * MLA_V
MLA_IN_PAD = MLA_Q_LORA + MLA_KV_LORA + HEAD_PAD
IN_PAD = 3 * HY_WIDTH + 4 * RET_W + MLA_IN_PAD
Q_SCALE = MLA_QK ** -0.5 * math.log2(math.e)
ATTN_TK = 1024
DFT = 128
VMEM_LIMIT = 56 * 1024 * 1024


def _params(sem, vmem=None):
    return pltpu.CompilerParams(dimension_semantics=sem, vmem_limit_bytes=vmem)


def _full(shape):
    n = len(shape)
    return pl.BlockSpec(shape, lambda *_: (0,) * n, pipeline_mode=pl.Buffered(1))


def _dot(a, b):
    return jnp.dot(a, b, preferred_element_type=F32)


def _rms(x, g):
    return x * lax.rsqrt(jnp.mean(x * x, axis=-1, keepdims=True) + EPS) * g


def _ada_kernel(c_ref, w_ref, b_ref, o_ref):
    s = jax.nn.silu(c_ref[...])
    o_ref[...] = _dot(s.astype(BF16), w_ref[...].astype(BF16)) + b_ref[...]


def _ada(cond8, w_ada, b_ada):
    depth, d, n = w_ada.shape
    tn = 1536
    return pl.pallas_call(
        _ada_kernel,
        grid=(depth, n // tn),
        in_specs=[pl.BlockSpec((8, d), lambda l, j: (0, 0)),
                  pl.BlockSpec((None, d, tn), lambda l, j: (l, 0, j)),
                  pl.BlockSpec((None, 1, tn), lambda l, j: (l, 0, j))],
        out_specs=pl.BlockSpec((None, 8, tn), lambda l, j: (l, 0, j)),
        out_shape=jax.ShapeDtypeStruct((depth, 8, n), F32),
        compiler_params=_params(("arbitrary", "arbitrary"), VMEM_LIMIT),
        name="ada",
    )(cond8, w_ada, b_ada.reshape(depth, 1, n))


def _rope(t, c, s1, s2):
    return t * c + pltpu.roll(t, LANE - 8, 1) * s1 + pltpu.roll(t, 8, 1) * s2


def _inproj_kernel(x_ref, sh_ref, sc_ref, g_ref, win_ref, qg_ref, kvg_ref, wuq_ref, wuk_ref, wuv_ref, vone_ref,
                   rc_ref, rs1_ref, rs2_ref,
                   ux1_ref, ux2_ref, uv_ref, rq_ref, rk_ref, rv_ref, rg_ref, mq_ref, mk_ref, mv_ref):
    w = HY_WIDTH
    h = (_rms(x_ref[...], g_ref[...]) * (1 + sc_ref[...]) + sh_ref[...]).astype(BF16)
    ux1_ref[...] = _dot(h, win_ref[:, 0:w])
    ux2_ref[...] = _dot(h, win_ref[:, w:2 * w])
    uv_ref[...] = _dot(h, win_ref[:, 2 * w:3 * w])
    o = 3 * w
    rq_ref[...] = _dot(h, win_ref[:, o:o + RET_W])
    rk_ref[...] = _dot(h, win_ref[:, o + RET_W:o + 2 * RET_W]) * RET_DK ** -0.5
    rv_ref[...] = _dot(h, win_ref[:, o + 2 * RET_W:o + 3 * RET_W])
    rg_ref[...] = _dot(h, win_ref[:, o + 3 * RET_W:o + 4 * RET_W])
    o = o + 4 * RET_W
    um = _dot(h, win_ref[:, o:o + MLA_IN_PAD])
    cq = um[:, :MLA_Q_LORA]
    ckv = um[:, MLA_Q_LORA:MLA_Q_LORA + MLA_KV_LORA]
    kr = um[:, MLA_Q_LORA + MLA_KV_LORA:]
    rc, rs1, rs2 = rc_ref[...], rs1_ref[...], rs2_ref[...]
    q = _dot(_rms(cq, qg_ref[...]).astype(BF16), wuq_ref[...])
    for hd in range(MLA_HEADS):
        sl = slice(hd * HEAD_PAD, (hd + 1) * HEAD_PAD)
        mq_ref[:, sl] = (_rope(q[:, sl], rc, rs1, rs2) * Q_SCALE).astype(BF16)
    ckvn = _rms(ckv, kvg_ref[...]).astype(BF16)
    kn = _dot(ckvn, wuk_ref[...])
    krr = _rope(kr, rc, rs1, rs2)
    for hd in range(MLA_HEADS):
        sl = slice(hd * HEAD_PAD, (hd + 1) * HEAD_PAD)
        mk_ref[:, sl] = (kn[:, sl] + krr).astype(BF16)
    mv_ref[...] = (_dot(ckvn, wuv_ref[...]) + vone_ref[...]).astype(BF16)


def _inproj(x, sh, sc, g, lw, rope_tabs, tm):
    b, t, d = x.shape
    tok = lambda n: pl.BlockSpec((None, tm, n), lambda i, j: (i, j, 0))
    mod = pl.BlockSpec((None, 1, d), lambda i, j: (i, 0, 0))
    tab = pl.BlockSpec((tm, LANE), lambda i, j: (j, 0))
    outs = [(HY_WIDTH, F32)] * 3 + [(RET_W, F32)] * 4 + [(MLA_QW, BF16)] * 3
    vone = jnp.zeros((MLA_HEADS, HEAD_PAD), F32).at[:, MLA_V].set(1.0).reshape(1, MLA_QW)
    return pl.pallas_call(
        _inproj_kernel,
        grid=(b, t // tm),
        in_specs=[tok(d), mod, mod, _full((1, d)), _full(lw["w_in"].shape), _full((1, MLA_Q_LORA)),
                  _full((1, MLA_KV_LORA)), _full(lw["w_uq"].shape), _full(lw["w_uk"].shape),
                  _full(lw["w_uv"].shape), _full((1, MLA_QW)), tab, tab, tab],
        out_specs=[tok(n) for n, _ in outs],
        out_shape=[jax.ShapeDtypeStruct((b, t, n), dt) for n, dt in outs],
        compiler_params=_params(("arbitrary", "arbitrary"), VMEM_LIMIT),
        name="inproj",
    )(x, sh, sc, g, lw["w_in"], lw["q_g"], lw["kv_g"], lw["w_uq"], lw["w_uk"], lw["w_uv"], vone, *rope_tabs)


def _sconv_kernel(u_ref, w_ref, b_ref, o_ref, *, rows):
    t, cw = u_ref.shape
    w = w_ref[...]
    bias = b_ref[...]
    first = lax.broadcasted_iota(jnp.int32, (rows, cw), 0) == 0
    last = lax.broadcasted_iota(jnp.int32, (rows, cw), 0) == rows - 1
    zero = jnp.zeros((1, cw), F32)
    for r0 in range(0, t, rows):
        cur = u_ref[r0:r0 + rows, :]
        prev = u_ref[r0 - 1:r0, :] if r0 > 0 else zero
        nxt = u_ref[r0 + rows:r0 + rows + 1, :] if r0 + rows < t else zero
        up = jnp.where(first, prev, pltpu.roll(cur, 1, 0))
        dn = jnp.where(last, nxt, pltpu.roll(cur, rows - 1, 0))
        o_ref[r0:r0 + rows, :] = bias + up * w[0:1] + cur * w[1:2] + dn * w[2:3]


def _sconv(u, w, bias):
    b, t, c = u.shape
    cw = LANE
    return pl.pallas_call(
        functools.partial(_sconv_kernel, rows=min(t, 512)),
        grid=(b, c // cw),
        in_specs=[pl.BlockSpec((None, t, cw), lambda i, j: (i, 0, j)),
                  pl.BlockSpec((3, cw), lambda i, j: (0, j)),
                  pl.BlockSpec((1, cw), lambda i, j: (0, j))],
        out_specs=pl.BlockSpec((None, t, cw), lambda i, j: (i, 0, j)),
        out_shape=jax.ShapeDtypeStruct((b, t, c), F32),
        compiler_params=_params(("arbitrary", "arbitrary"), VMEM_LIMIT),
        name="sconv",
    )(u, w, bias)


def _filt_kernel(z_ref, w1_ref, b1_ref, sf_ref, w2_ref, b2_ref, w3_ref, dl_ref, k_ref, nrm_ref, *, seq, rows):
    i = pl.program_id(0)
    hp = lax.Precision.HIGHEST
    z = z_ref[...]
    h = jnp.sin(sf_ref[0:1, :] * (jnp.dot(z, w1_ref[...], precision=hp, preferred_element_type=F32) + b1_ref[...]))
    h = jnp.sin(sf_ref[1:2, :] * (jnp.dot(h, w2_ref[...], precision=hp, preferred_element_type=F32) + b2_ref[...]))
    h3 = jnp.dot(h, w3_ref[...], precision=hp, preferred_element_type=F32)
    half = HY_ORDER * HY_WIDTH
    win = jnp.exp(-z[:, 0:1] * dl_ref[...])
    fwd = h3[:, :half] * win
    bwd = h3[:, half:] * win
    row = i * rows + lax.broadcasted_iota(jnp.int32, (rows, half), 0)
    k = jnp.where(row == 0, fwd + bwd, jnp.where(row < seq, fwd, jnp.where(row == seq, 0.0, bwd)))
    k_ref[...] = k

    @pl.when(i == 0)
    def _():
        nrm_ref[...] = jnp.zeros_like(nrm_ref)

    nrm_ref[...] += jnp.sum(jnp.abs(k), axis=0, keepdims=True)


def _hyena_filter(seq, fw):
    t = jnp.linspace(0.0, 1.0, seq, dtype=F32)[:, None]
    bands = jnp.linspace(1e-4, HY_BANDS - 1, HY_BANDS, dtype=F32)[None, :]
    ang = (2.0 * math.pi / seq) * jnp.arange(seq, dtype=F32)[:, None] * bands
    z = jnp.concatenate([t, jnp.cos(ang), -jnp.sin(ang)], axis=-1)
    z2 = jnp.concatenate([z, jnp.zeros((1, HY_EMB), F32), z[1:][::-1]], axis=0)
    z2 = jnp.pad(z2, ((0, 0), (0, LANE - HY_EMB)))
    max_decay = math.log(HY_TARGET) / HY_FAST_PCT
    min_decay = math.log(HY_TARGET) / HY_SLOW_PCT
    deltas = jnp.abs(jnp.linspace(min_decay, max_decay, HY_WIDTH, dtype=F32))
    dl = jnp.tile(deltas, HY_ORDER)[None, :]
    half = HY_ORDER * HY_WIDTH
    rows = min(2 * seq, 1024)
    return pl.pallas_call(
        functools.partial(_filt_kernel, seq=seq, rows=rows),
        grid=(2 * seq // rows,),
        in_specs=[pl.BlockSpec((rows, LANE), lambda i: (i, 0)), _full((LANE, LANE)), _full((1, LANE)),
                  _full((2, LANE)), _full((LANE, LANE)), _full((1, LANE)), _full((LANE, 2 * half)),
                  _full((1, half))],
        out_specs=[pl.BlockSpec((rows, half), lambda i: (i, 0)), pl.BlockSpec((1, half), lambda i: (0, 0))],
        out_shape=[jax.ShapeDtypeStruct((2 * seq, half), F32), jax.ShapeDtypeStruct((1, half), F32)],
        compiler_params=_params(("arbitrary",), VMEM_LIMIT),
        name="hyena_filter",
    )(z2, fw["w1"], fw["b1"], fw["sf"], fw["w2"], fw["b2"], fw["w3"], dl)


def _dft_consts():
    idx = np.arange(DFT)
    ang = 2.0 * np.pi * ((idx[:, None] * idx[None, :]) % DFT) / DFT
    fcat = np.concatenate([np.cos(ang), -np.sin(ang)], axis=0)
    n = DFT * DFT
    tang = 2.0 * np.pi * (idx[:, None] * idx[None, :]) / n
    return (jnp.asarray(fcat, F32), jnp.asarray(np.cos(tang), F32), jnp.asarray(-np.sin(tang), F32))


def _cmul(ar, ai, br, bi):
    return ar * br - ai * bi, ar * bi + ai * br


def _fft1_kernel(x_ref, f_ref, twr_ref, twi_ref, or_ref, oi_ref, *, n_b, cw, has_im):
    f = f_ref[...]
    if has_im:
        xr = jnp.swapaxes(x_ref[0], 0, 1)
        xi = jnp.swapaxes(x_ref[1], 0, 1)
    else:
        xr = jnp.swapaxes(x_ref[...], 0, 1)
    outs_r, outs_i = [], []
    for j in range(n_b):
        if has_im:
            o = _dot(f, jnp.concatenate([xr[j], xi[j]], axis=1).astype(BF16))
            ar = o[:DFT, :cw] - o[DFT:, cw:]
            ai = o[:DFT, cw:] + o[DFT:, :cw]
        else:
            o = _dot(f, xr[j].astype(BF16))
            ar, ai = o[:DFT], o[DFT:]
        tr = jnp.concatenate([twr_ref[j]] * (cw // LANE), axis=1)
        ti = jnp.concatenate([twi_ref[j]] * (cw // LANE), axis=1)
        pr, pi = _cmul(ar, ai, tr, ti)
        outs_r.append(pr.astype(BF16))
        outs_i.append(pi.astype(BF16))
    or_ref[...] = jnp.swapaxes(jnp.stack(outs_r, axis=0), 0, 1)
    oi_ref[...] = jnp.swapaxes(jnp.stack(outs_i, axis=0), 0, 1)


def _fft1(x, consts, n_a, cw, has_im, n_b=16):
    fcat, twr, twi = consts
    g = x.shape[0]
    f = fcat[:, :n_a].astype(BF16)
    if has_im:
        xspec = pl.BlockSpec((None, 2, n_a, n_b, cw), lambda j, i: (i, 0, 0, j, 0))
    else:
        xspec = pl.BlockSpec((None, n_a, n_b, cw), lambda j, i: (i, 0, j, 0))
    tw = pl.BlockSpec((n_b, DFT, LANE), lambda j, i: (j, 0, 0))
    ospec = pl.BlockSpec((None, DFT, n_b, cw), lambda j, i: (i, 0, j, 0))
    oshape = jax.ShapeDtypeStruct((g, DFT, DFT, cw), BF16)
    return pl.pallas_call(
        functools.partial(_fft1_kernel, n_b=n_b, cw=cw, has_im=has_im),
        grid=(DFT // n_b, g),
        in_specs=[xspec, _full(f.shape), tw, tw],
        out_specs=[ospec, ospec],
        out_shape=[oshape, oshape],
        compiler_params=_params(("arbitrary", "arbitrary"), VMEM_LIMIT),
        name="fft_stage1",
    )(x, f, twr, twi)


def _fft2_filter_kernel(ar_ref, ai_ref, f_ref, nrm_ref, kr_ref, ki_ref, *, qb, cw):
    f = f_ref[...]
    scale = 1.0 / (nrm_ref[...] * float(DFT * DFT))
    for qi in range(qb):
        o = _dot(f, jnp.concatenate([ar_ref[qi], ai_ref[qi]], axis=1))
        kr_ref[qi] = (o[:DFT, :cw] - o[DFT:, cw:]) * scale
        ki_ref[qi] = (o[:DFT, cw:] + o[DFT:, :cw]) * scale


def _fft2_filter(ar, ai, consts, nrm, cw, qb=4):
    fcat = consts[0].astype(BF16)
    spec = pl.BlockSpec((qb, DFT, cw), lambda i: (i, 0, 0))
    oshape = jax.ShapeDtypeStruct((DFT, DFT, cw), F32)
    return pl.pallas_call(
        functools.partial(_fft2_filter_kernel, qb=qb, cw=cw),
        grid=(DFT // qb,),
        in_specs=[spec, spec, _full(fcat.shape), _full((1, cw))],
        out_specs=[spec, spec],
        out_shape=[oshape, oshape],
        compiler_params=_params(("arbitrary",), VMEM_LIMIT),
        name="fft_filter_stage2",
    )(ar.reshape(DFT, DFT, cw), ai.reshape(DFT, DFT, cw), fcat, nrm)


def _fft2_kernel(ar_ref, ai_ref, f_ref, kr_ref, ki_ref, twr_ref, twi_ref, br_ref, bi_ref, *, qb, groups, cw):
    f = f_ref[...]
    for qi in range(qb):
        parts = []
        for g in range(groups):
            parts += [ar_ref[g, qi], ai_ref[g, qi]]
        o = _dot(f, jnp.concatenate(parts, axis=1))
        kr, ki = kr_ref[qi], ki_ref[qi]
        ys = []
        for g in range(groups):
            rr = slice(2 * g * cw, (2 * g + 1) * cw)
            ii = slice((2 * g + 1) * cw, (2 * g + 2) * cw)
            xr = o[:DFT, rr] - o[DFT:, ii]
            xi = o[:DFT, ii] + o[DFT:, rr]
            yr, yi = _cmul(xr, xi, kr, ki)
            ys += [yr.astype(BF16), yi.astype(BF16)]
        o2 = _dot(f, jnp.concatenate(ys, axis=1))
        tr, ti = twr_ref[qi], twi_ref[qi]
        for g in range(groups):
            rr = slice(2 * g * cw, (2 * g + 1) * cw)
            ii = slice((2 * g + 1) * cw, (2 * g + 2) * cw)
            br = o2[:DFT, rr] + o2[DFT:, ii]
            bi = o2[:DFT, ii] - o2[DFT:, rr]
            for s in range(cw // LANE):
                ls = slice(s * LANE, (s + 1) * LANE)
                pr, pi = _cmul(br[:, ls], bi[:, ls], tr, -ti)
                br_ref[g, qi, :, ls] = pr.astype(BF16)
                bi_ref[g, qi, :, ls] = pi.astype(BF16)


def _fft2(ar, ai, consts, kr, ki, order, cw, qb=4):
    fcat, twr, twi = consts
    fcat = fcat.astype(BF16)
    g = ar.shape[0]
    a4 = pl.BlockSpec((g, qb, DFT, cw), lambda i: (0, i, 0, 0))
    ksp = pl.BlockSpec((qb, DFT, cw), lambda i: (i, 0, order))
    tw = pl.BlockSpec((qb, DFT, LANE), lambda i: (i, 0, 0))
    oshape = jax.ShapeDtypeStruct((g, DFT, DFT, cw), BF16)
    return pl.pallas_call(
        functools.partial(_fft2_kernel, qb=qb, groups=g, cw=cw),
        grid=(DFT // qb,),
        in_specs=[a4, a4, _full(fcat.shape), ksp, ksp, tw, tw],
        out_specs=[a4, a4],
        out_shape=[oshape, oshape],
        compiler_params=_params(("arbitrary",), VMEM_LIMIT),
        name="fft_stage2",
    )(ar, ai, fcat, kr, ki, twr, twi)


def _fft3_kernel(br_ref, bi_ref, f_ref, z_ref, gate_ref, bias_ref, o_ref, *, n_b, cw, n_a):
    f = f_ref[...]
    br = jnp.swapaxes(br_ref[...], 0, 1)
    bi = jnp.swapaxes(bi_ref[...], 0, 1)
    y0, y1 = [], []
    for j in range(n_b):
        o = _dot(f, jnp.concatenate([br[j], bi[j]], axis=1))
        y0.append(o[:n_a, :cw] + o[n_a:, cw:])
        y1.append(o[:n_a, cw:] - o[n_a:, :cw])
    bias = bias_ref[...]
    for s, ys in enumerate((y0, y1)):
        y = jnp.swapaxes(jnp.stack(ys, axis=0), 0, 1)
        o_ref[s] = gate_ref[s] * (y + z_ref[s] * bias)


def _fft3(br, bi, consts, z, gate, bias, n_a, cw, n_b=16):
    fcat = consts[0]
    f = jnp.concatenate([fcat[:n_a], fcat[DFT:DFT + n_a]], axis=0).astype(BF16)
    g = br.shape[0]
    bsp = pl.BlockSpec((None, DFT, n_b, cw), lambda j, i: (i, 0, j, 0))
    zsp = pl.BlockSpec((None, 2, n_a, n_b, cw), lambda j, i: (i, 0, 0, j, 0))
    return pl.pallas_call(
        functools.partial(_fft3_kernel, n_b=n_b, cw=cw, n_a=n_a),
        grid=(DFT // n_b, g),
        in_specs=[bsp, bsp, _full(f.shape), zsp, zsp, _full((1, cw))],
        out_specs=zsp,
        out_shape=jax.ShapeDtypeStruct(z.shape, F32),
        compiler_params=_params(("arbitrary", "arbitrary"), VMEM_LIMIT),
        name="fft_stage3",
    )(br, bi, f, z, gate, bias)


def _hyena_long(x1, x2, v, fw, bias):
    b, seq, w = v.shape
    assert 2 * seq == DFT * DFT and b % 2 == 0
    n_a = seq // DFT
    consts = _dft_consts()
    fcat, twr, twi = consts
    consts = (fcat, jnp.broadcast_to(twr[:, :, None], (DFT, DFT, LANE)),
              jnp.broadcast_to(twi[:, :, None], (DFT, DFT, LANE)))
    kf, nrm = _hyena_filter(seq, fw)
    half = HY_ORDER * w
    far, fai = _fft1(kf.reshape(1, DFT, DFT, half), consts, DFT, half, False)
    kr, ki = _fft2_filter(far, fai, consts, nrm, half)
    view = lambda a: a.reshape(b // 2, 2, n_a, DFT, w)
    z, gates = view(v), (view(x1), view(x2))
    for o in range(HY_ORDER):
        ar, ai = _fft1(z, consts, n_a, w, True)
        br, bi = _fft2(ar, ai, consts, kr, ki, o, w)
        z = _fft3(br, bi, consts, z, gates[o], bias[o:o + 1], n_a, w)
    return z.reshape(b, seq, w)


def _ctx_hyena_kernel(x1_ref, x2_ref, v_ref, kf_ref, nrm_ref, fd_ref, fi_ref, bias_ref, o_ref, *, seq, w):
    n = 2 * seq
    fd = fd_ref[...]
    kc = _dot(fd, kf_ref[...].astype(BF16))
    scale = 1.0 / (nrm_ref[...] * float(n))
    kr, ki = kc[:n] * scale, kc[n:] * scale
    finv = fi_ref[...]
    z = v_ref[...]
    gates = (x1_ref[...], x2_ref[...])
    for o in range(HY_ORDER):
        sl = slice(o * w, (o + 1) * w)
        xf = _dot(fd[:, :seq], z.astype(BF16))
        yr, yi = _cmul(xf[:n], xf[n:], kr[:, sl], ki[:, sl])
        y = _dot(finv, jnp.concatenate([yr, yi], axis=0).astype(BF16))
        z = gates[o] * (y + z * bias_ref[o:o + 1, :])
    o_ref[...] = z


def _hyena_ctx(x1, x2, v, fw, bias):
    b, seq, w = v.shape
    n = 2 * seq
    idx = np.arange(n)
    ang = 2.0 * np.pi * ((idx[:, None] * idx[None, :]) % n) / n
    fr, fi = np.cos(ang), -np.sin(ang)
    fd = jnp.asarray(np.concatenate([fr, fi], axis=0), F32).astype(BF16)
    finv = jnp.asarray(np.concatenate([fr[:seq], fi[:seq]], axis=1), F32).astype(BF16)
    kf, nrm = _hyena_filter(seq, fw)
    tok = pl.BlockSpec((None, seq, w), lambda i: (i, 0, 0))
    return pl.pallas_call(
        functools.partial(_ctx_hyena_kernel, seq=seq, w=w),
        grid=(b,),
        in_specs=[tok, tok, tok, _full(kf.shape), _full(nrm.shape), _full(fd.shape), _full(finv.shape),
                  _full(bias.shape)],
        out_specs=tok,
        out_shape=jax.ShapeDtypeStruct(v.shape, F32),
        compiler_params=_params(("arbitrary",), VMEM_LIMIT),
        name="hyena_ctx",
    )(x1, x2, v, kf, nrm, fd, finv, bias)


def _ret_kernel(*refs, reverse, finalize, n_chunk):
    if finalize:
        q_ref, k_ref, v_ref, ld_ref, s0_ref, of_ref, g_ref, avg_ref, out_ref, sfin_ref, s_scr = refs
    else:
        q_ref, k_ref, v_ref, ld_ref, s0_ref, out_ref, sfin_ref, s_scr = refs
    c = q_ref.shape[0] // n_chunk
    t = pl.program_id(1)

    @pl.when(t == 0)
    def _():
        s_scr[...] = s0_ref[...]

    r = lax.broadcasted_iota(jnp.int32, (c, c), 0)
    m = lax.broadcasted_iota(jnp.int32, (c, c), 1)
    diff = ((m - r) if reverse else (r - m)).astype(F32)
    pos = lax.broadcasted_iota(jnp.int32, (c, RET_DV), 0).astype(F32)
    order = range(n_chunk - 1, -1, -1) if reverse else range(n_chunk)
    for hd in range(RET_HEADS):
        lg = jnp.log1p(-jnp.exp(ld_ref[hd]))
        lgv = lg[:, :RET_DV]
        lgc = jnp.concatenate([lg] * (c // LANE), axis=1)
        decay = jnp.where(diff >= 0, jnp.exp(lgc * jnp.maximum(diff, 0.0)), 0.0)
        if reverse:
            zeta = jnp.exp(lgv * pos)
            xi = jnp.exp(lgv * (c - pos))
        else:
            zeta = jnp.exp(lgv * (c - 1 - pos))
            xi = jnp.exp(lgv * (pos + 1))
        g_chunk = jnp.exp(lgv * c)
        hs = slice(hd * RET_DK, (hd + 1) * RET_DK)
        state = s_scr[hd]
        entering = {}
        for ci in order:
            rows = slice(ci * c, (ci + 1) * c)
            kv = lax.dot_general(k_ref[rows, hs].astype(BF16), (v_ref[rows, hs] * zeta).astype(BF16),
                                 (((0,), (0,)), ((), ())), preferred_element_type=F32)
            entering[ci] = state
            state = g_chunk * state + kv
        s_scr[hd] = state
        for ci in order:
            rows = slice(ci * c, (ci + 1) * c)
            qh = q_ref[rows, hs].astype(BF16)
            kh = k_ref[rows, hs].astype(BF16)
            s = lax.dot_general(qh, kh, (((1,), (1,)), ((), ())), preferred_element_type=F32)
            inner = _dot((s * decay).astype(BF16), v_ref[rows, hs].astype(BF16))
            o = inner + _dot(qh, entering[ci].astype(BF16)) * xi
            out_ref[rows, hs] = o

    if finalize:
        o = out_ref[...] + of_ref[...]
        ms = _dot((o * o).astype(BF16), avg_ref[...])
        out_ref[...] = jax.nn.silu(g_ref[...]) * (o * lax.rsqrt(ms + EPS))

    @pl.when(t == pl.num_programs(1) - 1)
    def _():
        sfin_ref[...] = s_scr[...]


def _ret_sweep(q, k, v, ld, s0, reverse, fwd_out=None, gate=None):
    b, t, w = q.shape
    tt = min(t, 1024)
    nt = t // tt
    finalize = fwd_out is not None
    tmap = (lambda i, j: (i, nt - 1 - j, 0)) if reverse else (lambda i, j: (i, j, 0))
    tok = pl.BlockSpec((None, tt, w), tmap)
    st = pl.BlockSpec((None, RET_HEADS, RET_DK, RET_DV), lambda i, j: (i, 0, 0, 0))
    ins = [q, k, v, ld, s0]
    in_specs = [tok, tok, tok, _full(ld.shape), st]
    if finalize:
        avg = jnp.kron(jnp.eye(RET_HEADS, dtype=F32), jnp.full((RET_DV, RET_DV), 1.0 / RET_DV, F32)).astype(BF16)
        ins += [fwd_out, gate, avg]
        in_specs += [tok, tok, _full(avg.shape)]
    return pl.pallas_call(
        functools.partial(_ret_kernel, reverse=reverse, finalize=finalize, n_chunk=tt // RET_KCHUNK),
        grid=(b, nt),
        in_specs=in_specs,
        out_specs=[tok, st],
        out_shape=[jax.ShapeDtypeStruct((b, t, w), F32),
                   jax.ShapeDtypeStruct((b, RET_HEADS, RET_DK, RET_DV), F32)],
        scratch_shapes=[pltpu.VMEM((RET_HEADS, RET_DK, RET_DV), F32)],
        compiler_params=_params(("arbitrary", "arbitrary"), VMEM_LIMIT),
        name="retention_bwd" if reverse else "retention_fwd",
    )(*ins)


def _retention(q, k, v, g, ld, s_f, s_b):
    out_f, fin_f = _ret_sweep(q, k, v, ld[0], s_f, False)
    out, fin_b = _ret_sweep(q, k, v, ld[1], s_b, True, out_f, g)
    return out, fin_f, fin_b


def _attn_kernel(*refs, n_src, tk):
    q_ref = refs[0]
    o_ref, s_buf, p_buf = refs[1 + 2 * n_src:]
    tq = q_ref.shape[0]
    chunks = []
    for si in range(n_src):
        k_ref, v_ref = refs[1 + 2 * si], refs[2 + 2 * si]
        rows = min(tk, k_ref.shape[0])
        chunks += [(k_ref, v_ref, r0, rows) for r0 in range(0, k_ref.shape[0], rows)]
    n = len(chunks)
    for hh in range(2):
        hs = slice(hh * HEAD_PAD, (hh + 1) * HEAD_PAD)
        q = q_ref[:, hs]

        def scores(c, slot):
            k_ref, _, r0, rows = chunks[c]
            s_buf[slot, :, :rows] = lax.dot_general(q, k_ref[r0:r0 + rows, hs], (((1,), (1,)), ((), ())),
                                                    preferred_element_type=F32)

        def pv(c, slot):
            _, v_ref, r0, rows = chunks[c]
            return _dot(p_buf[slot, :, :rows], v_ref[r0:r0 + rows, hs])

        scores(0, 0)
        m = jnp.full((tq, 1), -jnp.inf, F32)
        acc = jnp.zeros((tq, HEAD_PAD), F32)
        a_prev = None
        for c in range(n):
            slot = c % 2
            if c + 1 < n:
                scores(c + 1, 1 - slot)
            if c >= 1:
                acc = a_prev * acc + pv(c - 1, 1 - slot)
            rows = chunks[c][3]
            s = s_buf[slot, :, :rows]
            m_new = jnp.maximum(m, jnp.max(s, axis=-1, keepdims=True))
            a_prev = jnp.exp2(m - m_new)
            p_buf[slot, :, :rows] = jnp.exp2(s - m_new).astype(BF16)
            m = m_new
        acc = a_prev * acc + pv(n - 1, (n - 1) % 2)
        o_ref[:, hh * MLA_V:(hh + 1) * MLA_V] = (acc[:, :MLA_V] / acc[:, MLA_V:MLA_V + 1]).astype(o_ref.dtype)


def _attention(q, kvs, tq, tk):
    b, t, _ = q.shape
    in_specs = [pl.BlockSpec((None, tq, 2 * HEAD_PAD), lambda i, h, j: (i, j, h))]
    args = [q]
    for k, v in kvs:
        in_specs += [pl.BlockSpec((None, k.shape[1], 2 * HEAD_PAD), lambda i, h, j: (i, 0, h))] * 2
        args += [k, v]
    tk = min(tk, max(k.shape[1] for k, _ in kvs))
    return pl.pallas_call(
        functools.partial(_attn_kernel, n_src=len(kvs), tk=tk),
        grid=(b, MLA_HEADS // 2, t // tq),
        in_specs=in_specs,
        out_specs=pl.BlockSpec((None, tq, 2 * MLA_V), lambda i, h, j: (i, j, h)),
        out_shape=jax.ShapeDtypeStruct((b, t, MLA_VW), BF16),
        scratch_shapes=[pltpu.VMEM((2, tq, tk), F32), pltpu.VMEM((2, tq, tk), BF16)],
        compiler_params=_params(("arbitrary", "arbitrary", "arbitrary"), VMEM_LIMIT),
        name="mla_attention",
    )(*args)


def _outmlp_kernel(*refs, final, ffc):
    if final:
        (x_ref, hy_ref, ret_ref, att_ref, ga1_ref, sh2_ref, sc2_ref, ga2_ref, g2_ref, wo_ref, w1_ref, w2_ref,
         fg_ref, o_ref) = refs
    else:
        (x_ref, hy_ref, ret_ref, att_ref, ga1_ref, sh2_ref, sc2_ref, ga2_ref, g2_ref, wo_ref, w1_ref, w2_ref,
         o_ref) = refs
    w = HY_WIDTH
    mix = (_dot(hy_ref[...].astype(BF16), wo_ref[0:w, :]) + _dot(ret_ref[...].astype(BF16), wo_ref[w:2 * w, :])
           + _dot(att_ref[...], wo_ref[2 * w:, :]))
    x = x_ref[...] + ga1_ref[...] * mix
    h = (_rms(x, g2_ref[...]) * (1 + sc2_ref[...]) + sh2_ref[...]).astype(BF16)
    acc = jnp.zeros(x.shape, F32)
    for c0 in range(0, D_FF, ffc):
        hid = jnp.square(jax.nn.relu(_dot(h, w1_ref[:, c0:c0 + ffc]))).astype(BF16)
        acc = acc + _dot(hid, w2_ref[c0:c0 + ffc, :])
    x = x + ga2_ref[...] * acc
    if final:
        x = _rms(x, fg_ref[...])
    o_ref[...] = x


def _outmlp(x, hy, ret, att, ga1, sh2, sc2, ga2, lw, final_g, tm):
    b, t, d = x.shape
    tok = lambda n: pl.BlockSpec((None, tm, n), lambda i, j: (i, j, 0))
    mod = pl.BlockSpec((None, 1, d), lambda i, j: (i, 0, 0))
    final = final_g is not None
    ins = [x, hy, ret, att, ga1, sh2, sc2, ga2, lw["norm2_g"], lw["w_out"], lw["w1"], lw["w2"]]
    in_specs = [tok(d), tok(hy.shape[-1]), tok(ret.shape[-1]), tok(att.shape[-1]), mod, mod, mod, mod,
                _full((1, d)), _full(lw["w_out"].shape), _full(lw["w1"].shape), _full(lw["w2"].shape)]
    if final:
        ins.append(final_g)
        in_specs.append(_full((1, d)))
    return pl.pallas_call(
        functools.partial(_outmlp_kernel, final=final, ffc=1024),
        grid=(b, t // tm),
        in_specs=in_specs,
        out_specs=tok(d),
        out_shape=jax.ShapeDtypeStruct((b, t, d), F32),
        compiler_params=_params(("arbitrary", "arbitrary"), VMEM_LIMIT),
        name="outproj_mlp",
    )(*ins)


def _rope_tables(t):
    rows = t // GRID_W
    row = jnp.repeat(jnp.arange(rows), GRID_W).astype(F32)
    col = jnp.tile(jnp.arange(GRID_W), rows).astype(F32)
    n_freq = MLA_ROPE // 4
    inv = ROPE_BASE ** (-jnp.arange(n_freq, dtype=F32) / n_freq)
    ang_r = row[:, None] * inv
    ang_c = col[:, None] * inv
    cr, sr, cc, sc = jnp.cos(ang_r), jnp.sin(ang_r), jnp.cos(ang_c), jnp.sin(ang_c)
    one = jnp.ones((t, MLA_NOPE), F32)
    z64 = jnp.zeros((t, MLA_NOPE), F32)
    z8 = jnp.zeros((t, n_freq), F32)
    tail1 = jnp.ones((t, HEAD_PAD - MLA_QK), F32)
    tail0 = jnp.zeros((t, HEAD_PAD - MLA_QK), F32)
    c = jnp.concatenate([one, cr, cr, cc, cc, tail1], axis=1)
    s1 = jnp.concatenate([z64, -sr, z8, -sc, z8, tail0], axis=1)
    s2 = jnp.concatenate([z64, z8, sr, z8, sc, tail0], axis=1)
    return c, s1, s2


def _identity_rope_tables(t):
    return jnp.ones((t, HEAD_PAD), F32), jnp.zeros((t, HEAD_PAD), F32), jnp.zeros((t, HEAD_PAD), F32)


def _layer_weights(p):
    d = D_MODEL
    w_in = p["w_in"]
    hy_cols = 3 * HY_WIDTH
    ret_cols = 4 * RET_W
    w_mla = w_in[:, hy_cols + ret_cols:]
    pad_l = jnp.zeros((d, MLA_NOPE), F32)
    pad_r = jnp.zeros((d, HEAD_PAD - MLA_QK), F32)
    w_mla = jnp.concatenate([w_mla[:, :MLA_Q_LORA + MLA_KV_LORA], pad_l, w_mla[:, MLA_Q_LORA + MLA_KV_LORA:], pad_r],
                            axis=1)
    w_in_pad = jnp.concatenate([w_in[:, :hy_cols + ret_cols], w_mla], axis=1).astype(BF16)
    w_uq = p["mla_w_uq"].reshape(MLA_Q_LORA, MLA_HEADS, MLA_QK)
    w_uq = jnp.pad(w_uq, ((0, 0), (0, 0), (0, HEAD_PAD - MLA_QK))).reshape(MLA_Q_LORA, MLA_QW).astype(BF16)
    w_ukv = p["mla_w_ukv"].reshape(MLA_KV_LORA, MLA_HEADS, MLA_NOPE + MLA_V)
    w_uk = jnp.pad(w_ukv[:, :, :MLA_NOPE], ((0, 0), (0, 0), (0, HEAD_PAD - MLA_NOPE)))
    w_uk = w_uk.reshape(MLA_KV_LORA, MLA_QW).astype(BF16)
    w_uv = jnp.pad(w_ukv[:, :, MLA_NOPE:], ((0, 0), (0, 0), (0, HEAD_PAD - MLA_V)))
    w_uv = w_uv.reshape(MLA_KV_LORA, MLA_QW).astype(BF16)
    pad_f = LANE - HY_FFN
    fw = {
        "w1": jnp.pad(p["hy_ffn_w1"], ((0, LANE - HY_EMB), (0, pad_f))),
        "b1": jnp.pad(p["hy_ffn_b1"], (0, pad_f))[None, :],
        "sf": jnp.pad(p["hy_sin_freq"], ((0, 0), (0, pad_f))),
        "w2": jnp.pad(p["hy_ffn_w2"], ((0, pad_f), (0, pad_f))),
        "b2": jnp.pad(p["hy_ffn_b2"], (0, pad_f))[None, :],
        "w3": jnp.pad(p["hy_ffn_w3"], ((0, pad_f), (0, 0))),
    }
    ld = jnp.broadcast_to(p["ret_log_decay"][:, :, None, None], (2, RET_HEADS, 1, LANE))
    return {
        "w_in": w_in_pad, "w_uq": w_uq, "w_uk": w_uk, "w_uv": w_uv,
        "q_g": p["mla_q_norm_g"][None, :], "kv_g": p["mla_kv_norm_g"][None, :],
        "norm1_g": p["norm1_g"][None, :], "norm2_g": p["norm2_g"][None, :],
        "w_out": p["w_out"].astype(BF16), "w1": p["mlp_w1"].astype(BF16), "w2": p["mlp_w2"].astype(BF16),
        "conv_w": p["hy_conv_w"], "conv_b": p["hy_conv_b"][None, :], "hy_bias": p["hy_bias"],
        "filt": fw, "ld": ld,
    }


def _mixers(x, sh1, sc1, lw, rope_tabs, tm):
    ux1, ux2, uv, rq, rk, rv, rg, mq, mk, mv = _inproj(x, sh1, sc1, lw["norm1_g"], lw, rope_tabs, tm)
    w = HY_WIDTH
    cw, cb = lw["conv_w"], lw["conv_b"]
    x1 = _sconv(ux1, cw[:, 0:w], cb[:, 0:w])
    x2 = _sconv(ux2, cw[:, w:2 * w], cb[:, w:2 * w])
    v = _sconv(uv, cw[:, 2 * w:], cb[:, 2 * w:])
    return (x1, x2, v), (rq, rk, rv, rg), (mq, mk, mv)


def kernel(x, c, ctx, c_ctx, w_ada, b_ada, norm1_g, norm2_g, w_in, w_out, hy_conv_w, hy_conv_b, hy_ffn_w1,
           hy_ffn_b1, hy_sin_freq, hy_ffn_w2, hy_ffn_b2, hy_ffn_w3, hy_bias, ret_log_decay, mla_q_norm_g, mla_w_uq,
           mla_kv_norm_g, mla_w_ukv, mlp_w1, mlp_w2, final_norm_g):
    b, seq, d = x.shape
    n_ctx = ctx.shape[1]
    depth = w_ada.shape[0]
    cond8 = jnp.concatenate([c, c_ctx[None, :], jnp.zeros((8 - b - 1, d), F32)], axis=0)
    ada = _ada(cond8, w_ada, b_ada)
    rope_lat = _rope_tables(seq)
    rope_ctx = _identity_rope_tables(n_ctx)
    zero_state = jnp.zeros((b, RET_HEADS, RET_DK, RET_DV), F32)
    xc = ctx
    for i in range(depth):
        p = {
            "w_in": w_in[i], "w_out": w_out[i], "norm1_g": norm1_g[i], "norm2_g": norm2_g[i],
            "hy_conv_w": hy_conv_w[i], "hy_conv_b": hy_conv_b[i], "hy_ffn_w1": hy_ffn_w1[i],
            "hy_ffn_b1": hy_ffn_b1[i], "hy_sin_freq": hy_sin_freq[i], "hy_ffn_w2": hy_ffn_w2[i],
            "hy_ffn_b2": hy_ffn_b2[i], "hy_ffn_w3": hy_ffn_w3[i], "hy_bias": hy_bias[i],
            "ret_log_decay": ret_log_decay[i], "mla_q_norm_g": mla_q_norm_g[i], "mla_w_uq": mla_w_uq[i],
            "mla_kv_norm_g": mla_kv_norm_g[i], "mla_w_ukv": mla_w_ukv[i], "mlp_w1": mlp_w1[i], "mlp_w2": mlp_w2[i],
        }
        lw = _layer_weights(p)
        last = i == depth - 1
        terms = [ada[i, :, k * d:(k + 1) * d] for k in range(N_MOD)]
        lat = [tm[:b, None, :] for tm in terms]
        cx = [jnp.broadcast_to(tm[b:b + 1, None, :], (b, 1, d)) for tm in terms]

        hy_c, ret_c, (cq, ck, cv) = _mixers(xc, cx[0], cx[1], lw, rope_ctx, n_ctx)
        hy_l, ret_l, (mq, mk, mv) = _mixers(x, lat[0], lat[1], lw, rope_lat, 512)

        rq, rk, rv, rg = ret_c
        ret_ctx, s_f, s_b = _retention(rq, rk, rv, rg, lw["ld"], zero_state, zero_state)
        rq, rk, rv, rg = ret_l
        ret, _, _ = _retention(rq, rk, rv, rg, lw["ld"], s_f, s_b)

        hy = _hyena_long(*hy_l, lw["filt"], lw["hy_bias"])

        att = _attention(mq, [(mk, mv), (ck, cv)], 256, ATTN_TK)

        fg = final_norm_g[None, :] if last else None
        x = _outmlp(x, hy, ret, att, lat[2], lat[3], lat[4], lat[5], lw, fg, 512)
        if not last:
            hyc = _hyena_ctx(*hy_c, lw["filt"], lw["hy_bias"])
            att_c = _attention(cq, [(ck, cv)], n_ctx, ATTN_TK)
            xc = _outmlp(xc, hyc, ret_ctx, att_c, cx[2], cx[3], cx[4], cx[5], lw, None, n_ctx)
    return x
```

```python
import functools
import math

import numpy as np
import jax
import jax.numpy as jnp
from jax import lax
from jax.experimental import pallas as pl
from jax.experimental.pallas import tpu as pltpu

F32 = jnp.float32
BF16 = jnp.bfloat16

D_MODEL = 1024
DEPTH = 2
GRID_W = 64
D_FF = 4 * D_MODEL
N_MOD = 6
EPS = 1e-6

HY_WIDTH = D_MODEL // 4
HY_ORDER = 2
HY_EMB = 33
HY_BANDS = (HY_EMB - 1) // 2
HY_FFN = 64
HY_TARGET = 1e-2
HY_FAST_PCT = 0.3
HY_SLOW_PCT = 1.5

RET_DK = 64
RET_DV = 64
RET_HEADS = (D_MODEL // 4) // RET_DV
RET_KCHUNK = 256
RET_W = RET_HEADS * RET_DK

MLA_V = 64
MLA_HEADS = (D_MODEL // 2) // MLA_V
MLA_NOPE = 64
MLA_ROPE = 32
MLA_QK = MLA_NOPE + MLA_ROPE
MLA_Q_LORA = D_MODEL // 4
MLA_KV_LORA = D_MODEL // 8
ROPE_BASE = 10000.0

LANE = 128
HEAD_PAD = LANE
MLA_QW = MLA_HEADS * HEAD_PAD
MLA_VW = MLA_HEADS * MLA_V
MLA_IN_PAD = MLA_Q_LORA + MLA_KV_LORA + HEAD_PAD
IN_PAD = 3 * HY_WIDTH + 4 * RET_W + MLA_IN_PAD
Q_SCALE = MLA_QK ** -0.5 * math.log2(math.e)
ATTN_TK = 1024
DFT = 128
VMEM_LIMIT = 56 * 1024 * 1024


def _params(sem, vmem=None):
    return pltpu.CompilerParams(dimension_semantics=sem, vmem_limit_bytes=vmem)


def _full(shape):
    n = len(shape)
    return pl.BlockSpec(shape, lambda *_: (0,) * n, pipeline_mode=pl.Buffered(1))


def _dot(a, b):
    return jnp.dot(a, b, preferred_element_type=F32)


def _rms(x, g):
    return x * lax.rsqrt(jnp.mean(x * x, axis=-1, keepdims=True) + EPS) * g


def _ada_kernel(c_ref, w_ref, b_ref, o_ref):
    s = jax.nn.silu(c_ref[...])
    o_ref[...] = _dot(s.astype(BF16), w_ref[...].astype(BF16)) + b_ref[...]


def _ada(cond8, w_ada, b_ada):
    depth, d, n = w_ada.shape
    tn = 1536
    return pl.pallas_call(
        _ada_kernel,
        grid=(depth, n // tn),
        in_specs=[pl.BlockSpec((8, d), lambda l, j: (0, 0)),
                  pl.BlockSpec((None, d, tn), lambda l, j: (l, 0, j)),
                  pl.BlockSpec((None, 1, tn), lambda l, j: (l, 0, j))],
        out_specs=pl.BlockSpec((None, 8, tn), lambda l, j: (l, 0, j)),
        out_shape=jax.ShapeDtypeStruct((depth, 8, n), F32),
        compiler_params=_params(("arbitrary", "arbitrary"), VMEM_LIMIT),
        name="ada",
    )(cond8, w_ada, b_ada.reshape(depth, 1, n))


def _rope(t, c, s1, s2):
    return t * c + pltpu.roll(t, LANE - 8, 1) * s1 + pltpu.roll(t, 8, 1) * s2


def _inproj_kernel(x_ref, sh_ref, sc_ref, g_ref, win_ref, qg_ref, kvg_ref, wuq_ref, wuk_ref, wuv_ref, vone_ref,
                   rc_ref, rs1_ref, rs2_ref,
                   ux1_ref, ux2_ref, uv_ref, rq_ref, rk_ref, rv_ref, rg_ref, mq_ref, mk_ref, mv_ref):
    w = HY_WIDTH
    h = (_rms(x_ref[...], g_ref[...]) * (1 + sc_ref[...]) + sh_ref[...]).astype(BF16)
    ux1_ref[...] = _dot(h, win_ref[:, 0:w])
    ux2_ref[...] = _dot(h, win_ref[:, w:2 * w])
    uv_ref[...] = _dot(h, win_ref[:, 2 * w:3 * w])
    o = 3 * w
    rq_ref[...] = _dot(h, win_ref[:, o:o + RET_W]).astype(BF16)
    rk_ref[...] = (_dot(h, win_ref[:, o + RET_W:o + 2 * RET_W]) * RET_DK ** -0.5).astype(BF16)
    rv_ref[...] = _dot(h, win_ref[:, o + 2 * RET_W:o + 3 * RET_W]).astype(BF16)
    rg_ref[...] = _dot(h, win_ref[:, o + 3 * RET_W:o + 4 * RET_W])
    o = o + 4 * RET_W
    um = _dot(h, win_ref[:, o:o + MLA_IN_PAD])
    cq = um[:, :MLA_Q_LORA]
    ckv = um[:, MLA_Q_LORA:MLA_Q_LORA + MLA_KV_LORA]
    kr = um[:, MLA_Q_LORA + MLA_KV_LORA:]
    rc, rs1, rs2 = rc_ref[...], rs1_ref[...], rs2_ref[...]
    q = _dot(_rms(cq, qg_ref[...]).astype(BF16), wuq_ref[...])
    for hd in range(MLA_HEADS):
        sl = slice(hd * HEAD_PAD, (hd + 1) * HEAD_PAD)
        mq_ref[:, sl] = (_rope(q[:, sl], rc, rs1, rs2) * Q_SCALE).astype(BF16)
    ckvn = _rms(ckv, kvg_ref[...]).astype(BF16)
    kn = _dot(ckvn, wuk_ref[...])
    krr = _rope(kr, rc, rs1, rs2)
    for hd in range(MLA_HEADS):
        sl = slice(hd * HEAD_PAD, (hd + 1) * HEAD_PAD)
        mk_ref[:, sl] = (kn[:, sl] + krr).astype(BF16)
    mv_ref[...] = (_dot(ckvn, wuv_ref[...]) + vone_ref[...]).astype(BF16)


def _inproj(x, sh, sc, g, lw, rope_tabs, tm):
    b, t, d = x.shape
    tok = lambda n: pl.BlockSpec((None, tm, n), lambda i, j: (i, j, 0))
    mod = pl.BlockSpec((None, 1, d), lambda i, j: (i, 0, 0))
    tab = pl.BlockSpec((tm, LANE), lambda i, j: (j, 0))
    outs = [(HY_WIDTH, F32)] * 3 + [(RET_W, BF16)] * 3 + [(RET_W, F32)] + [(MLA_QW, BF16)] * 3
    vone = jnp.zeros((MLA_HEADS, HEAD_PAD), F32).at[:, MLA_V].set(1.0).reshape(1, MLA_QW)
    return pl.pallas_call(
        _inproj_kernel,
        grid=(b, t // tm),
        in_specs=[tok(d), mod, mod, _full((1, d)), _full(lw["w_in"].shape), _full((1, MLA_Q_LORA)),
                  _full((1, MLA_KV_LORA)), _full(lw["w_uq"].shape), _full(lw["w_uk"].shape),
                  _full(lw["w_uv"].shape), _full((1, MLA_QW)), tab, tab, tab],
        out_specs=[tok(n) for n, _ in outs],
        out_shape=[jax.ShapeDtypeStruct((b, t, n), dt) for n, dt in outs],
        compiler_params=_params(("arbitrary", "arbitrary"), VMEM_LIMIT),
        name="inproj",
    )(x, sh, sc, g, lw["w_in"], lw["q_g"], lw["kv_g"], lw["w_uq"], lw["w_uk"], lw["w_uv"], vone, *rope_tabs)


def _sconv_kernel(u_ref, w_ref, b_ref, o_ref, *, rows):
    t, cw = u_ref.shape
    w = w_ref[...]
    bias = b_ref[...]
    first = lax.broadcasted_iota(jnp.int32, (rows, cw), 0) == 0
    last = lax.broadcasted_iota(jnp.int32, (rows, cw), 0) == rows - 1
    zero = jnp.zeros((1, cw), F32)
    for r0 in range(0, t, rows):
        cur = u_ref[r0:r0 + rows, :]
        prev = u_ref[r0 - 1:r0, :] if r0 > 0 else zero
        nxt = u_ref[r0 + rows:r0 + rows + 1, :] if r0 + rows < t else zero
        up = jnp.where(first, prev, pltpu.roll(cur, 1, 0))
        dn = jnp.where(last, nxt, pltpu.roll(cur, rows - 1, 0))
        o_ref[r0:r0 + rows, :] = bias + up * w[0:1] + cur * w[1:2] + dn * w[2:3]


def _sconv(u, w, bias):
    b, t, c = u.shape
    cw = LANE
    return pl.pallas_call(
        functools.partial(_sconv_kernel, rows=min(t, 512)),
        grid=(b, c // cw),
        in_specs=[pl.BlockSpec((None, t, cw), lambda i, j: (i, 0, j)),
                  pl.BlockSpec((3, cw), lambda i, j: (0, j)),
                  pl.BlockSpec((1, cw), lambda i, j: (0, j))],
        out_specs=pl.BlockSpec((None, t, cw), lambda i, j: (i, 0, j)),
        out_shape=jax.ShapeDtypeStruct((b, t, c), F32),
        compiler_params=_params(("arbitrary", "arbitrary"), VMEM_LIMIT),
        name="sconv",
    )(u, w, bias)


def _filt_kernel(z_ref, w1_ref, b1_ref, sf_ref, w2_ref, b2_ref, w3a_ref, w3b_ref, dl_ref, k_ref, nrm_ref, *, rows):
    i = pl.program_id(0)
    hp = lax.Precision.HIGHEST
    z = z_ref[...]
    h = jnp.sin(sf_ref[0:1, :] * (jnp.dot(z, w1_ref[...], precision=hp, preferred_element_type=F32) + b1_ref[...]))
    h = jnp.sin(sf_ref[1:2, :] * (jnp.dot(h, w2_ref[...], precision=hp, preferred_element_type=F32) + b2_ref[...]))
    ha = jnp.dot(h, w3a_ref[...], precision=hp, preferred_element_type=F32)
    hb = jnp.dot(h, w3b_ref[...], precision=hp, preferred_element_type=F32)
    half = HY_ORDER * HY_WIDTH
    win_a = jnp.exp(-z[:, 0:1] * dl_ref[...])
    win_b = jnp.exp(-z[:, LANE:LANE + 1] * dl_ref[...])
    first = (i * rows + lax.broadcasted_iota(jnp.int32, (rows, half), 0)) == 0
    fwd = ha[:, :half] * win_a
    top = jnp.where(first, fwd + ha[:, half:] * win_a, fwd)
    bot = jnp.where(first, 0.0, hb * win_b)
    k_ref[0] = top
    k_ref[1] = bot

    @pl.when(i == 0)
    def _():
        nrm_ref[...] = jnp.zeros_like(nrm_ref)

    nrm_ref[...] += jnp.sum(jnp.abs(top), axis=0, keepdims=True) + jnp.sum(jnp.abs(bot), axis=0, keepdims=True)


def _hyena_filter(seq, fw):
    t = jnp.linspace(0.0, 1.0, seq, dtype=F32)[:, None]
    bands = jnp.linspace(1e-4, HY_BANDS - 1, HY_BANDS, dtype=F32)[None, :]
    ang = (2.0 * math.pi / seq) * jnp.arange(seq, dtype=F32)[:, None] * bands
    z = jnp.concatenate([t, jnp.cos(ang), -jnp.sin(ang)], axis=-1)
    zneg = jnp.concatenate([jnp.zeros((1, HY_EMB), F32), z[1:][::-1]], axis=0)
    pad = ((0, 0), (0, LANE - HY_EMB))
    z2 = jnp.concatenate([jnp.pad(z, pad), jnp.pad(zneg, pad)], axis=1)
    max_decay = math.log(HY_TARGET) / HY_FAST_PCT
    min_decay = math.log(HY_TARGET) / HY_SLOW_PCT
    deltas = jnp.abs(jnp.linspace(min_decay, max_decay, HY_WIDTH, dtype=F32))
    dl = jnp.tile(deltas, HY_ORDER)[None, :]
    half = HY_ORDER * HY_WIDTH
    rows = min(seq, 512)
    k, nrm = pl.pallas_call(
        functools.partial(_filt_kernel, rows=rows),
        grid=(seq // rows,),
        in_specs=[pl.BlockSpec((rows, 2 * LANE), lambda i: (i, 0)), _full((2 * LANE, LANE)), _full((1, LANE)),
                  _full((2, LANE)), _full((LANE, LANE)), _full((1, LANE)), _full((LANE, 2 * half)),
                  _full((LANE, half)), _full((1, half))],
        out_specs=[pl.BlockSpec((2, rows, half), lambda i: (0, i, 0)), pl.BlockSpec((1, half), lambda i: (0, 0))],
        out_shape=[jax.ShapeDtypeStruct((2, seq, half), F32), jax.ShapeDtypeStruct((1, half), F32)],
        compiler_params=_params(("arbitrary",), VMEM_LIMIT),
        name="hyena_filter",
    )(z2, fw["w1"], fw["b1"], fw["sf"], fw["w2"], fw["b2"], fw["w3a"], fw["w3b"], dl)
    return k.reshape(2 * seq, half), nrm


def _dft_consts(n_a):
    idx = np.arange(DFT)
    ang = 2.0 * np.pi * ((idx[:, None] * idx[None, :]) % DFT) / DFT
    fr, fi = jnp.asarray(np.cos(ang), F32), jnp.asarray(-np.sin(ang), F32)
    tang = 2.0 * np.pi * (idx[:, None] * idx[None, :]) / (DFT * DFT)
    tr, ti = jnp.asarray(np.cos(tang), F32), jnp.asarray(-np.sin(tang), F32)
    c = tr[:, :, None] * fr[None, :, :n_a] - ti[:, :, None] * fi[None, :, :n_a]
    sn = tr[:, :, None] * fi[None, :, :n_a] + ti[:, :, None] * fr[None, :, :n_a]
    f1 = jnp.concatenate([c, sn], axis=1).astype(BF16)
    f3 = jnp.concatenate([jnp.swapaxes(c, 1, 2), jnp.swapaxes(sn, 1, 2)], axis=1).astype(BF16)
    return jnp.concatenate([fr, fi], axis=0).astype(BF16), f1, f3


def _cmul(ar, ai, br, bi):
    return ar * br - ai * bi, ar * bi + ai * br


def _fft1_kernel(x_ref, f_ref, or_ref, oi_ref, *, n_b, cw, has_im):
    if has_im:
        xr = jnp.swapaxes(x_ref[0], 0, 1)
        xi = jnp.swapaxes(x_ref[1], 0, 1)
    else:
        xr = jnp.swapaxes(x_ref[...], 0, 1)
    outs_r, outs_i = [], []
    for j in range(n_b):
        if has_im:
            o = _dot(f_ref[j], jnp.concatenate([xr[j], xi[j]], axis=1).astype(BF16))
            ar = o[:DFT, :cw] - o[DFT:, cw:]
            ai = o[:DFT, cw:] + o[DFT:, :cw]
        else:
            o = _dot(f_ref[j], xr[j].astype(BF16))
            ar, ai = o[:DFT], o[DFT:]
        outs_r.append(ar.astype(BF16))
        outs_i.append(ai.astype(BF16))
    or_ref[...] = jnp.swapaxes(jnp.stack(outs_r, axis=0), 0, 1)
    oi_ref[...] = jnp.swapaxes(jnp.stack(outs_i, axis=0), 0, 1)


def _fft1(x, f1, cw, has_im, n_b=16):
    g = x.shape[0]
    n_a = f1.shape[2]
    if has_im:
        xspec = pl.BlockSpec((None, 2, n_a, n_b, cw), lambda j, i: (i, 0, 0, j, 0))
    else:
        xspec = pl.BlockSpec((None, n_a, n_b, cw), lambda j, i: (i, 0, j, 0))
    ospec = pl.BlockSpec((None, DFT, n_b, cw), lambda j, i: (i, 0, j, 0))
    oshape = jax.ShapeDtypeStruct((g, DFT, DFT, cw), BF16)
    return pl.pallas_call(
        functools.partial(_fft1_kernel, n_b=n_b, cw=cw, has_im=has_im),
        grid=(DFT // n_b, g),
        in_specs=[xspec, pl.BlockSpec((n_b, 2 * DFT, n_a), lambda j, i: (j, 0, 0))],
        out_specs=[ospec, ospec],
        out_shape=[oshape, oshape],
        compiler_params=_params(("arbitrary", "arbitrary"), VMEM_LIMIT),
        name="fft_stage1",
    )(x, f1)


def _fft2_filter_kernel(ar_ref, ai_ref, f_ref, nrm_ref, kr_ref, ki_ref, *, qb, cw):
    f = f_ref[...]
    scale = 1.0 / (nrm_ref[...] * float(DFT * DFT))
    for qi in range(qb):
        o = _dot(f, jnp.concatenate([ar_ref[qi], ai_ref[qi]], axis=1))
        kr_ref[qi] = (o[:DFT, :cw] - o[DFT:, cw:]) * scale
        ki_ref[qi] = (o[:DFT, cw:] + o[DFT:, :cw]) * scale


def _fft2_filter(ar, ai, fcat, nrm, cw, qb=4):
    spec = pl.BlockSpec((qb, DFT, cw), lambda i: (i, 0, 0))
    oshape = jax.ShapeDtypeStruct((DFT, DFT, cw), F32)
    return pl.pallas_call(
        functools.partial(_fft2_filter_kernel, qb=qb, cw=cw),
        grid=(DFT // qb,),
        in_specs=[spec, spec, _full(fcat.shape), _full((1, cw))],
        out_specs=[spec, spec],
        out_shape=[oshape, oshape],
        compiler_params=_params(("arbitrary",), VMEM_LIMIT),
        name="fft_filter_stage2",
    )(ar.reshape(DFT, DFT, cw), ai.reshape(DFT, DFT, cw), fcat, nrm)


def _fft2_kernel(ar_ref, ai_ref, f_ref, kr_ref, ki_ref, br_ref, bi_ref, *, qb, groups, cw):
    f = f_ref[...]
    for qi in range(qb):
        parts = []
        for g in range(groups):
            parts += [ar_ref[g, qi], ai_ref[g, qi]]
        o = _dot(f, jnp.concatenate(parts, axis=1))
        kr, ki = kr_ref[qi], ki_ref[qi]
        ys = []
        for g in range(groups):
            rr = slice(2 * g * cw, (2 * g + 1) * cw)
            ii = slice((2 * g + 1) * cw, (2 * g + 2) * cw)
            xr = o[:DFT, rr] - o[DFT:, ii]
            xi = o[:DFT, ii] + o[DFT:, rr]
            yr, yi = _cmul(xr, xi, kr, ki)
            ys += [yr.astype(BF16), yi.astype(BF16)]
        o2 = _dot(f, jnp.concatenate(ys, axis=1))
        for g in range(groups):
            rr = slice(2 * g * cw, (2 * g + 1) * cw)
            ii = slice((2 * g + 1) * cw, (2 * g + 2) * cw)
            br_ref[g, qi] = (o2[:DFT, rr] + o2[DFT:, ii]).astype(BF16)
            bi_ref[g, qi] = (o2[:DFT, ii] - o2[DFT:, rr]).astype(BF16)


def _fft2(ar, ai, fcat, kr, ki, order, cw, qb=8):
    g = ar.shape[0]
    a4 = pl.BlockSpec((g, qb, DFT, cw), lambda i: (0, i, 0, 0))
    ksp = pl.BlockSpec((qb, DFT, cw), lambda i: (i, 0, order))
    oshape = jax.ShapeDtypeStruct((g, DFT, DFT, cw), BF16)
    return pl.pallas_call(
        functools.partial(_fft2_kernel, qb=qb, groups=g, cw=cw),
        grid=(DFT // qb,),
        in_specs=[a4, a4, _full(fcat.shape), ksp, ksp],
        out_specs=[a4, a4],
        out_shape=[oshape, oshape],
        compiler_params=_params(("arbitrary",), VMEM_LIMIT),
        name="fft_stage2",
    )(ar, ai, fcat, kr, ki)


def _fft3_kernel(br_ref, bi_ref, f_ref, z_ref, gate_ref, bias_ref, o_ref, *, n_b, cw, n_a):
    br = jnp.swapaxes(br_ref[...], 0, 1)
    bi = jnp.swapaxes(bi_ref[...], 0, 1)
    y0, y1 = [], []
    for j in range(n_b):
        o = _dot(f_ref[j], jnp.concatenate([br[j], bi[j]], axis=1))
        y0.append(o[:n_a, :cw] + o[n_a:, cw:])
        y1.append(o[:n_a, cw:] - o[n_a:, :cw])
    bias = bias_ref[...]
    for s, ys in enumerate((y0, y1)):
        y = jnp.swapaxes(jnp.stack(ys, axis=0), 0, 1)
        o_ref[s] = gate_ref[s] * (y + z_ref[s] * bias)


def _fft3(br, bi, f3, z, gate, bias, cw, n_b=16):
    g = br.shape[0]
    n_a = f3.shape[1] // 2
    bsp = pl.BlockSpec((None, DFT, n_b, cw), lambda j, i: (i, 0, j, 0))
    zsp = pl.BlockSpec((None, 2, n_a, n_b, cw), lambda j, i: (i, 0, 0, j, 0))
    return pl.pallas_call(
        functools.partial(_fft3_kernel, n_b=n_b, cw=cw, n_a=n_a),
        grid=(DFT // n_b, g),
        in_specs=[bsp, bsp, pl.BlockSpec((n_b, 2 * n_a, DFT), lambda j, i: (j, 0, 0)), zsp, zsp, _full((1, cw))],
        out_specs=zsp,
        out_shape=jax.ShapeDtypeStruct(z.shape, F32),
        compiler_params=_params(("arbitrary", "arbitrary"), VMEM_LIMIT),
        name="fft_stage3",
    )(br, bi, f3, z, gate, bias)


def _hyena_long(x1, x2, v, fw, bias):
    b, seq, w = v.shape
    assert 2 * seq == DFT * DFT and b % 2 == 0
    n_a = seq // DFT
    fcat, f1_full, _ = _dft_consts(DFT)
    _, f1, f3 = _dft_consts(n_a)
    kf, nrm = _hyena_filter(seq, fw)
    half = HY_ORDER * w
    far, fai = _fft1(kf.reshape(1, DFT, DFT, half), f1_full, half, False)
    kr, ki = _fft2_filter(far, fai, fcat, nrm, half)
    view = lambda a: a.reshape(b // 2, 2, n_a, DFT, w)
    z, gates = view(v), (view(x1), view(x2))
    for o in range(HY_ORDER):
        ar, ai = _fft1(z, f1, w, True)
        br, bi = _fft2(ar, ai, fcat, kr, ki, o, w)
        z = _fft3(br, bi, f3, z, gates[o], bias[o:o + 1], w)
    return z.reshape(b, seq, w)


def _ctx_hyena_kernel(x1_ref, x2_ref, v_ref, kf_ref, nrm_ref, fd_ref, fi_ref, bias_ref, o_ref, *, seq, w):
    n = 2 * seq
    fd = fd_ref[...]
    kc = _dot(fd, kf_ref[...].astype(BF16))
    scale = 1.0 / (nrm_ref[...] * float(n))
    kr, ki = kc[:n] * scale, kc[n:] * scale
    finv = fi_ref[...]
    z = v_ref[...]
    gates = (x1_ref[...], x2_ref[...])
    for o in range(HY_ORDER):
        sl = slice(o * w, (o + 1) * w)
        xf = _dot(fd[:, :seq], z.astype(BF16))
        yr, yi = _cmul(xf[:n], xf[n:], kr[:, sl], ki[:, sl])
        y = _dot(finv, jnp.concatenate([yr, yi], axis=0).astype(BF16))
        z = gates[o] * (y + z * bias_ref[o:o + 1, :])
    o_ref[...] = z


def _hyena_ctx(x1, x2, v, fw, bias):
    b, seq, w = v.shape
    n = 2 * seq
    idx = np.arange(n)
    ang = 2.0 * np.pi * ((idx[:, None] * idx[None, :]) % n) / n
    fr, fi = np.cos(ang), -np.sin(ang)
    fd = jnp.asarray(np.concatenate([fr, fi], axis=0), F32).astype(BF16)
    finv = jnp.asarray(np.concatenate([fr[:seq], fi[:seq]], axis=1), F32).astype(BF16)
    kf, nrm = _hyena_filter(seq, fw)
    tok = pl.BlockSpec((None, seq, w), lambda i: (i, 0, 0))
    return pl.pallas_call(
        functools.partial(_ctx_hyena_kernel, seq=seq, w=w),
        grid=(b,),
        in_specs=[tok, tok, tok, _full(kf.shape), _full(nrm.shape), _full(fd.shape), _full(finv.shape),
                  _full(bias.shape)],
        out_specs=tok,
        out_shape=jax.ShapeDtypeStruct(v.shape, F32),
        compiler_params=_params(("arbitrary",), VMEM_LIMIT),
        name="hyena_ctx",
    )(x1, x2, v, kf, nrm, fd, finv, bias)


def _ret_kernel(*refs, reverse, finalize, n_chunk):
    if finalize:
        q_ref, k_ref, v_ref, ld_ref, s0_ref, of_ref, g_ref, avg_ref, out_ref, sfin_ref, s_scr = refs
    else:
        q_ref, k_ref, v_ref, ld_ref, s0_ref, out_ref, sfin_ref, s_scr = refs
    c = q_ref.shape[0] // n_chunk
    t = pl.program_id(1)

    @pl.when(t == 0)
    def _():
        s_scr[...] = s0_ref[...]

    r = lax.broadcasted_iota(jnp.int32, (c, c), 0)
    m = lax.broadcasted_iota(jnp.int32, (c, c), 1)
    diff = ((m - r) if reverse else (r - m)).astype(F32)
    pos = lax.broadcasted_iota(jnp.int32, (c, RET_DV), 0).astype(F32)
    order = range(n_chunk - 1, -1, -1) if reverse else range(n_chunk)
    for hd in range(RET_HEADS):
        lg = jnp.log1p(-jnp.exp(ld_ref[hd]))
        lgv = lg[:, :RET_DV]
        lgc = jnp.concatenate([lg] * (c // LANE), axis=1)
        decay = jnp.where(diff >= 0, jnp.exp(lgc * jnp.maximum(diff, 0.0)), 0.0)
        if reverse:
            zeta = jnp.exp(lgv * pos)
            xi = jnp.exp(lgv * (c - pos))
        else:
            zeta = jnp.exp(lgv * (c - 1 - pos))
            xi = jnp.exp(lgv * (pos + 1))
        g_chunk = jnp.exp(lgv * c)
        hs = slice(hd * RET_DK, (hd + 1) * RET_DK)
        state = s_scr[hd]
        entering = {}
        for ci in order:
            rows = slice(ci * c, (ci + 1) * c)
            kv = lax.dot_general(k_ref[rows, hs].astype(BF16), (v_ref[rows, hs] * zeta).astype(BF16),
                                 (((0,), (0,)), ((), ())), preferred_element_type=F32)
            entering[ci] = state
            state = g_chunk * state + kv
        s_scr[hd] = state
        for ci in order:
            rows = slice(ci * c, (ci + 1) * c)
            qh = q_ref[rows, hs].astype(BF16)
            kh = k_ref[rows, hs].astype(BF16)
            s = lax.dot_general(qh, kh, (((1,), (1,)), ((), ())), preferred_element_type=F32)
            inner = _dot((s * decay).astype(BF16), v_ref[rows, hs].astype(BF16))
            o = inner + _dot(qh, entering[ci].astype(BF16)) * xi
            out_ref[rows, hs] = o

    if finalize:
        o = out_ref[...] + of_ref[...]
        ms = _dot((o * o).astype(BF16), avg_ref[...])
        out_ref[...] = jax.nn.silu(g_ref[...]) * (o * lax.rsqrt(ms + EPS))

    @pl.when(t == pl.num_programs(1) - 1)
    def _():
        sfin_ref[...] = s_scr[...]


def _ret_sweep(q, k, v, ld, s0, reverse, fwd_out=None, gate=None):
    b, t, w = q.shape
    tt = min(t, 1024)
    nt = t // tt
    finalize = fwd_out is not None
    tmap = (lambda i, j: (i, nt - 1 - j, 0)) if reverse else (lambda i, j: (i, j, 0))
    tok = pl.BlockSpec((None, tt, w), tmap)
    st = pl.BlockSpec((None, RET_HEADS, RET_DK, RET_DV), lambda i, j: (i, 0, 0, 0))
    ins = [q, k, v, ld, s0]
    in_specs = [tok, tok, tok, _full(ld.shape), st]
    if finalize:
        avg = jnp.kron(jnp.eye(RET_HEADS, dtype=F32), jnp.full((RET_DV, RET_DV), 1.0 / RET_DV, F32)).astype(BF16)
        ins += [fwd_out, gate, avg]
        in_specs += [tok, tok, _full(avg.shape)]
    return pl.pallas_call(
        functools.partial(_ret_kernel, reverse=reverse, finalize=finalize, n_chunk=tt // RET_KCHUNK),
        grid=(b, nt),
        in_specs=in_specs,
        out_specs=[tok, st],
        out_shape=[jax.ShapeDtypeStruct((b, t, w), F32),
                   jax.ShapeDtypeStruct((b, RET_HEADS, RET_DK, RET_DV), F32)],
        scratch_shapes=[pltpu.VMEM((RET_HEADS, RET_DK, RET_DV), F32)],
        compiler_params=_params(("arbitrary", "arbitrary"), VMEM_LIMIT),
        name="retention_bwd" if reverse else "retention_fwd",
    )(*ins)


def _retention(q, k, v, g, ld, s_f, s_b):
    out_f, fin_f = _ret_sweep(q, k, v, ld[0], s_f, False)
    out, fin_b = _ret_sweep(q, k, v, ld[1], s_b, True, out_f, g)
    return out, fin_f, fin_b


def _attn_kernel(*refs, n_src, tk):
    q_ref = refs[0]
    o_ref, s_buf, p_buf = refs[1 + 2 * n_src:]
    tq = q_ref.shape[0]
    chunks = []
    for si in range(n_src):
        k_ref, v_ref = refs[1 + 2 * si], refs[2 + 2 * si]
        rows = min(tk, k_ref.shape[0])
        chunks += [(k_ref, v_ref, r0, rows) for r0 in range(0, k_ref.shape[0], rows)]
    n = len(chunks)
    for hh in range(2):
        hs = slice(hh * HEAD_PAD, (hh + 1) * HEAD_PAD)
        q = q_ref[:, hs]

        def scores(c, slot):
            k_ref, _, r0, rows = chunks[c]
            s_buf[slot, :, :rows] = lax.dot_general(q, k_ref[r0:r0 + rows, hs], (((1,), (1,)), ((), ())),
                                                    preferred_element_type=F32)

        def pv(c, slot):
            _, v_ref, r0, rows = chunks[c]
            return _dot(p_buf[slot, :, :rows], v_ref[r0:r0 + rows, hs])

        scores(0, 0)
        m = jnp.full((tq, 1), -jnp.inf, F32)
        acc = jnp.zeros((tq, HEAD_PAD), F32)
        a_prev = None
        for c in range(n):
            slot = c % 2
            if c + 1 < n:
                scores(c + 1, 1 - slot)
            if c >= 1:
                acc = a_prev * acc + pv(c - 1, 1 - slot)
            rows = chunks[c][3]
            s = s_buf[slot, :, :rows]
            m_new = jnp.maximum(m, jnp.max(s, axis=-1, keepdims=True))
            a_prev = jnp.exp2(m - m_new)
            p_buf[slot, :, :rows] = jnp.exp2(s - m_new).astype(BF16)
            m = m_new
        acc = a_prev * acc + pv(n - 1, (n - 1) % 2)
        o_ref[:, hh * MLA_V:(hh + 1) * MLA_V] = (acc[:, :MLA_V] / acc[:, MLA_V:MLA_V + 1]).astype(o_ref.dtype)


def _attention(q, kvs, tq, tk):
    b, t, _ = q.shape
    in_specs = [pl.BlockSpec((None, tq, 2 * HEAD_PAD), lambda i, h, j: (i, j, h))]
    args = [q]
    for k, v in kvs:
        in_specs += [pl.BlockSpec((None, k.shape[1], 2 * HEAD_PAD), lambda i, h, j: (i, 0, h))] * 2
        args += [k, v]
    tk = min(tk, max(k.shape[1] for k, _ in kvs))
    return pl.pallas_call(
        functools.partial(_attn_kernel, n_src=len(kvs), tk=tk),
        grid=(b, MLA_HEADS // 2, t // tq),
        in_specs=in_specs,
        out_specs=pl.BlockSpec((None, tq, 2 * MLA_V), lambda i, h, j: (i, j, h)),
        out_shape=jax.ShapeDtypeStruct((b, t, MLA_VW), BF16),
        scratch_shapes=[pltpu.VMEM((2, tq, tk), F32), pltpu.VMEM((2, tq, tk), BF16)],
        compiler_params=_params(("arbitrary", "arbitrary", "arbitrary"), VMEM_LIMIT),
        name="mla_attention",
    )(*args)


def _outmlp_kernel(*refs, final, ffc):
    if final:
        (x_ref, hy_ref, ret_ref, att_ref, ga1_ref, sh2_ref, sc2_ref, ga2_ref, g2_ref, wo_ref, w1_ref, w2_ref,
         fg_ref, o_ref) = refs
    else:
        (x_ref, hy_ref, ret_ref, att_ref, ga1_ref, sh2_ref, sc2_ref, ga2_ref, g2_ref, wo_ref, w1_ref, w2_ref,
         o_ref) = refs
    w = HY_WIDTH
    mix = (_dot(hy_ref[...].astype(BF16), wo_ref[0:w, :]) + _dot(ret_ref[...].astype(BF16), wo_ref[w:2 * w, :])
           + _dot(att_ref[...], wo_ref[2 * w:, :]))
    x = x_ref[...] + ga1_ref[...] * mix
    h = (_rms(x, g2_ref[...]) * (1 + sc2_ref[...]) + sh2_ref[...]).astype(BF16)
    acc = jnp.zeros(x.shape, F32)
    for c0 in range(0, D_FF, ffc):
        hid = jnp.square(jax.nn.relu(_dot(h, w1_ref[:, c0:c0 + ffc]))).astype(BF16)
        acc = acc + _dot(hid, w2_ref[c0:c0 + ffc, :])
    x = x + ga2_ref[...] * acc
    if final:
        x = _rms(x, fg_ref[...])
    o_ref[...] = x


def _outmlp(x, hy, ret, att, ga1, sh2, sc2, ga2, lw, final_g, tm):
    b, t, d = x.shape
    tok = lambda n: pl.BlockSpec((None, tm, n), lambda i, j: (i, j, 0))
    mod = pl.BlockSpec((None, 1, d), lambda i, j: (i, 0, 0))
    final = final_g is not None
    ins = [x, hy, ret, att, ga1, sh2, sc2, ga2, lw["norm2_g"], lw["w_out"], lw["w1"], lw["w2"]]
    in_specs = [tok(d), tok(hy.shape[-1]), tok(ret.shape[-1]), tok(att.shape[-1]), mod, mod, mod, mod,
                _full((1, d)), _full(lw["w_out"].shape), _full(lw["w1"].shape), _full(lw["w2"].shape)]
    if final:
        ins.append(final_g)
        in_specs.append(_full((1, d)))
    return pl.pallas_call(
        functools.partial(_outmlp_kernel, final=final, ffc=1024),
        grid=(b, t // tm),
        in_specs=in_specs,
        out_specs=tok(d),
        out_shape=jax.ShapeDtypeStruct((b, t, d), F32),
        compiler_params=_params(("arbitrary", "arbitrary"), VMEM_LIMIT),
        name="outproj_mlp",
    )(*ins)


def _rope_tables(t):
    rows = t // GRID_W
    row = jnp.repeat(jnp.arange(rows), GRID_W).astype(F32)
    col = jnp.tile(jnp.arange(GRID_W), rows).astype(F32)
    n_freq = MLA_ROPE // 4
    inv = ROPE_BASE ** (-jnp.arange(n_freq, dtype=F32) / n_freq)
    ang_r = row[:, None] * inv
    ang_c = col[:, None] * inv
    cr, sr, cc, sc = jnp.cos(ang_r), jnp.sin(ang_r), jnp.cos(ang_c), jnp.sin(ang_c)
    one = jnp.ones((t, MLA_NOPE), F32)
    z64 = jnp.zeros((t, MLA_NOPE), F32)
    z8 = jnp.zeros((t, n_freq), F32)
    tail1 = jnp.ones((t, HEAD_PAD - MLA_QK), F32)
    tail0 = jnp.zeros((t, HEAD_PAD - MLA_QK), F32)
    c = jnp.concatenate([one, cr, cr, cc, cc, tail1], axis=1)
    s1 = jnp.concatenate([z64, -sr, z8, -sc, z8, tail0], axis=1)
    s2 = jnp.concatenate([z64, z8, sr, z8, sc, tail0], axis=1)
    return c, s1, s2


def _identity_rope_tables(t):
    return jnp.ones((t, HEAD_PAD), F32), jnp.zeros((t, HEAD_PAD), F32), jnp.zeros((t, HEAD_PAD), F32)


def _layer_weights(p):
    d = D_MODEL
    w_in = p["w_in"]
    hy_cols = 3 * HY_WIDTH
    ret_cols = 4 * RET_W
    w_mla = w_in[:, hy_cols + ret_cols:]
    pad_l = jnp.zeros((d, MLA_NOPE), F32)
    pad_r = jnp.zeros((d, HEAD_PAD - MLA_QK), F32)
    w_mla = jnp.concatenate([w_mla[:, :MLA_Q_LORA + MLA_KV_LORA], pad_l, w_mla[:, MLA_Q_LORA + MLA_KV_LORA:], pad_r],
                            axis=1)
    w_in_pad = jnp.concatenate([w_in[:, :hy_cols + ret_cols], w_mla], axis=1).astype(BF16)
    w_uq = p["mla_w_uq"].reshape(MLA_Q_LORA, MLA_HEADS, MLA_QK)
    w_uq = jnp.pad(w_uq, ((0, 0), (0, 0), (0, HEAD_PAD - MLA_QK))).reshape(MLA_Q_LORA, MLA_QW).astype(BF16)
    w_ukv = p["mla_w_ukv"].reshape(MLA_KV_LORA, MLA_HEADS, MLA_NOPE + MLA_V)
    w_uk = jnp.pad(w_ukv[:, :, :MLA_NOPE], ((0, 0), (0, 0), (0, HEAD_PAD - MLA_NOPE)))
    w_uk = w_uk.reshape(MLA_KV_LORA, MLA_QW).astype(BF16)
    w_uv = jnp.pad(w_ukv[:, :, MLA_NOPE:], ((0, 0), (0, 0), (0, HEAD_PAD - MLA_V)))
    w_uv = w_uv.reshape(MLA_KV_LORA, MLA_QW).astype(BF16)
    f = HY_FFN
    half = HY_ORDER * HY_WIDTH
    two = lambda a: jnp.concatenate([a, a], axis=-1)
    w1 = jnp.zeros((2 * LANE, LANE), F32)
    w1 = w1.at[:HY_EMB, :f].set(p["hy_ffn_w1"]).at[LANE:LANE + HY_EMB, f:].set(p["hy_ffn_w1"])
    w2 = jnp.zeros((LANE, LANE), F32).at[:f, :f].set(p["hy_ffn_w2"]).at[f:, f:].set(p["hy_ffn_w2"])
    fw = {
        "w1": w1, "b1": two(p["hy_ffn_b1"])[None, :], "sf": two(p["hy_sin_freq"]),
        "w2": w2, "b2": two(p["hy_ffn_b2"])[None, :],
        "w3a": jnp.pad(p["hy_ffn_w3"], ((0, f), (0, 0))),
        "w3b": jnp.pad(p["hy_ffn_w3"][:, half:], ((f, 0), (0, 0))),
    }
    ld = jnp.broadcast_to(p["ret_log_decay"][:, :, None, None], (2, RET_HEADS, 1, LANE))
    return {
        "w_in": w_in_pad, "w_uq": w_uq, "w_uk": w_uk, "w_uv": w_uv,
        "q_g": p["mla_q_norm_g"][None, :], "kv_g": p["mla_kv_norm_g"][None, :],
        "norm1_g": p["norm1_g"][None, :], "norm2_g": p["norm2_g"][None, :],
        "w_out": p["w_out"].astype(BF16), "w1": p["mlp_w1"].astype(BF16), "w2": p["mlp_w2"].astype(BF16),
        "conv_w": p["hy_conv_w"], "conv_b": p["hy_conv_b"][None, :], "hy_bias": p["hy_bias"],
        "filt": fw, "ld": ld,
    }


def _mixers(x, sh1, sc1, lw, rope_tabs, tm):
    ux1, ux2, uv, rq, rk, rv, rg, mq, mk, mv = _inproj(x, sh1, sc1, lw["norm1_g"], lw, rope_tabs, tm)
    w = HY_WIDTH
    cw, cb = lw["conv_w"], lw["conv_b"]
    x1 = _sconv(ux1, cw[:, 0:w], cb[:, 0:w])
    x2 = _sconv(ux2, cw[:, w:2 * w], cb[:, w:2 * w])
    v = _sconv(uv, cw[:, 2 * w:], cb[:, 2 * w:])
    return (x1, x2, v), (rq, rk, rv, rg), (mq, mk, mv)


def kernel(x, c, ctx, c_ctx, w_ada, b_ada, norm1_g, norm2_g, w_in, w_out, hy_conv_w, hy_conv_b, hy_ffn_w1,
           hy_ffn_b1, hy_sin_freq, hy_ffn_w2, hy_ffn_b2, hy_ffn_w3, hy_bias, ret_log_decay, mla_q_norm_g, mla_w_uq,
           mla_kv_norm_g, mla_w_ukv, mlp_w1, mlp_w2, final_norm_g):
    b, seq, d = x.shape
    n_ctx = ctx.shape[1]
    depth = w_ada.shape[0]
    cond8 = jnp.concatenate([c, c_ctx[None, :], jnp.zeros((8 - b - 1, d), F32)], axis=0)
    ada = _ada(cond8, w_ada, b_ada)
    rope_lat = _rope_tables(seq)
    rope_ctx = _identity_rope_tables(n_ctx)
    zero_state = jnp.zeros((b, RET_HEADS, RET_DK, RET_DV), F32)
    xc = ctx
    for i in range(depth):
        p = {
            "w_in": w_in[i], "w_out": w_out[i], "norm1_g": norm1_g[i], "norm2_g": norm2_g[i],
            "hy_conv_w": hy_conv_w[i], "hy_conv_b": hy_conv_b[i], "hy_ffn_w1": hy_ffn_w1[i],
            "hy_ffn_b1": hy_ffn_b1[i], "hy_sin_freq": hy_sin_freq[i], "hy_ffn_w2": hy_ffn_w2[i],
            "hy_ffn_b2": hy_ffn_b2[i], "hy_ffn_w3": hy_ffn_w3[i], "hy_bias": hy_bias[i],
            "ret_log_decay": ret_log_decay[i], "mla_q_norm_g": mla_q_norm_g[i], "mla_w_uq": mla_w_uq[i],
            "mla_kv_norm_g": mla_kv_norm_g[i], "mla_w_ukv": mla_w_ukv[i], "mlp_w1": mlp_w1[i], "mlp_w2": mlp_w2[i],
        }
        lw = _layer_weights(p)
        last = i == depth - 1
        terms = [ada[i, :, k * d:(k + 1) * d] for k in range(N_MOD)]
        lat = [tm[:b, None, :] for tm in terms]
        cx = [jnp.broadcast_to(tm[b:b + 1, None, :], (b, 1, d)) for tm in terms]

        hy_c, ret_c, (cq, ck, cv) = _mixers(xc, cx[0], cx[1], lw, rope_ctx, n_ctx)
        hy_l, ret_l, (mq, mk, mv) = _mixers(x, lat[0], lat[1], lw, rope_lat, 1024)

        rq, rk, rv, rg = ret_c
        ret_ctx, s_f, s_b = _retention(rq, rk, rv, rg, lw["ld"], zero_state, zero_state)
        rq, rk, rv, rg = ret_l
        ret, _, _ = _retention(rq, rk, rv, rg, lw["ld"], s_f, s_b)

        hy = _hyena_long(*hy_l, lw["filt"], lw["hy_bias"])

        att = _attention(mq, [(mk, mv), (ck, cv)], 512, ATTN_TK)

        fg = final_norm_g[None, :] if last else None
        x = _outmlp(x, hy, ret, att, lat[2], lat[3], lat[4], lat[5], lw, fg, 512)
        if not last:
            hyc = _hyena_ctx(*hy_c, lw["filt"], lw["hy_bias"])
            att_c = _attention(cq, [(ck, cv)], n_ctx, ATTN_TK)
            xc = _outmlp(xc, hyc, ret_ctx, att_c, cx[2], cx[3], cx[4], cx[5], lw, None, n_ctx)
    return x
```

```python
import functools
import math

import numpy as np
import jax
import jax.numpy as jnp
from jax import lax
from jax.experimental import pallas as pl
from jax.experimental.pallas import tpu as pltpu

F32 = jnp.float32
BF16 = jnp.bfloat16

D_MODEL = 1024
DEPTH = 2
GRID_W = 64
D_FF = 4 * D_MODEL
N_MOD = 6
EPS = 1e-6

HY_WIDTH = D_MODEL // 4
HY_ORDER = 2
HY_EMB = 33
HY_BANDS = (HY_EMB - 1) // 2
HY_FFN = 64
HY_TARGET = 1e-2
HY_FAST_PCT = 0.3
HY_SLOW_PCT = 1.5

RET_DK = 64
RET_DV = 64
RET_HEADS = (D_MODEL // 4) // RET_DV
RET_KCHUNK = 256
RET_W = RET_HEADS * RET_DK

MLA_V = 64
MLA_HEADS = (D_MODEL // 2) // MLA_V
MLA_NOPE = 64
MLA_ROPE = 32
MLA_QK = MLA_NOPE + MLA_ROPE
MLA_Q_LORA = D_MODEL // 4
MLA_KV_LORA = D_MODEL // 8
ROPE_BASE = 10000.0

LANE = 128
HEAD_PAD = LANE
MLA_QW = MLA_HEADS * HEAD_PAD
MLA_VW = MLA_HEADS * MLA_V
MLA_IN_PAD = MLA_Q_LORA + MLA_KV_LORA + HEAD_PAD
IN_PAD = 3 * HY_WIDTH + 4 * RET_W + MLA_IN_PAD
Q_SCALE = MLA_QK ** -0.5 * math.log2(math.e)
ATTN_TK = 1024
DFT = 128
VMEM_LIMIT = 56 * 1024 * 1024


def _params(sem, vmem=None):
    return pltpu.CompilerParams(dimension_semantics=sem, vmem_limit_bytes=vmem)


def _full(shape):
    n = len(shape)
    return pl.BlockSpec(shape, lambda *_: (0,) * n, pipeline_mode=pl.Buffered(1))


def _dot(a, b):
    return jnp.dot(a, b, preferred_element_type=F32)


def _rms(x, g):
    return x * lax.rsqrt(jnp.mean(x * x, axis=-1, keepdims=True) + EPS) * g


def _ada_kernel(c_ref, w_ref, b_ref, o_ref):
    s = jax.nn.silu(c_ref[...])
    o_ref[...] = _dot(s.astype(BF16), w_ref[...].astype(BF16)) + b_ref[...]


def _ada(cond8, w_ada, b_ada):
    depth, d, n = w_ada.shape
    tn = 1536
    return pl.pallas_call(
        _ada_kernel,
        grid=(depth, n // tn),
        in_specs=[pl.BlockSpec((8, d), lambda l, j: (0, 0)),
                  pl.BlockSpec((None, d, tn), lambda l, j: (l, 0, j)),
                  pl.BlockSpec((None, 1, tn), lambda l, j: (l, 0, j))],
        out_specs=pl.BlockSpec((None, 8, tn), lambda l, j: (l, 0, j)),
        out_shape=jax.ShapeDtypeStruct((depth, 8, n), F32),
        compiler_params=_params(("arbitrary", "arbitrary"), VMEM_LIMIT),
        name="ada",
    )(cond8, w_ada, b_ada.reshape(depth, 1, n))


def _rope(t, c, s1, s2):
    return t * c + pltpu.roll(t, LANE - 8, 1) * s1 + pltpu.roll(t, 8, 1) * s2


def _inproj_kernel(x_ref, xp_ref, xn_ref, sh_ref, sc_ref, g_ref, win_ref, cw_ref, cb_ref, qg_ref, kvg_ref, wuq_ref,
                   wuk_ref, wuv_ref, vone_ref, rc_ref, rs1_ref, rs2_ref,
                   x1_ref, x2_ref, v_ref, rq_ref, rk_ref, rv_ref, rg_ref, mq_ref, mk_ref, mv_ref):
    w = HY_WIDTH
    j = pl.program_id(1)
    tm = x_ref.shape[0]
    mod = lambda t: (_rms(t, g_ref[...]) * (1 + sc_ref[...]) + sh_ref[...]).astype(BF16)
    h = mod(x_ref[...])
    edge = _dot(mod(jnp.concatenate([xp_ref[...], xn_ref[...]], axis=0)), win_ref[:, 0:3 * w])
    before = jnp.where(j > 0, edge[7:8], 0.0)
    after = jnp.where(j < pl.num_programs(1) - 1, edge[8:9], 0.0)
    first = lax.broadcasted_iota(jnp.int32, (tm, w), 0) == 0
    last = lax.broadcasted_iota(jnp.int32, (tm, w), 0) == tm - 1
    for gi, o_ref in enumerate((x1_ref, x2_ref, v_ref)):
        cs = slice(gi * w, (gi + 1) * w)
        u = _dot(h, win_ref[:, cs])
        up = jnp.where(first, before[:, cs], pltpu.roll(u, 1, 0))
        dn = jnp.where(last, after[:, cs], pltpu.roll(u, tm - 1, 0))
        o_ref[...] = cb_ref[:, cs] + up * cw_ref[0:1, cs] + u * cw_ref[1:2, cs] + dn * cw_ref[2:3, cs]
    o = 3 * w
    rq_ref[...] = _dot(h, win_ref[:, o:o + RET_W]).astype(BF16)
    rk_ref[...] = (_dot(h, win_ref[:, o + RET_W:o + 2 * RET_W]) * RET_DK ** -0.5).astype(BF16)
    rv_ref[...] = _dot(h, win_ref[:, o + 2 * RET_W:o + 3 * RET_W]).astype(BF16)
    rg_ref[...] = _dot(h, win_ref[:, o + 3 * RET_W:o + 4 * RET_W])
    o = o + 4 * RET_W
    um = _dot(h, win_ref[:, o:o + MLA_IN_PAD])
    cq = um[:, :MLA_Q_LORA]
    ckv = um[:, MLA_Q_LORA:MLA_Q_LORA + MLA_KV_LORA]
    kr = um[:, MLA_Q_LORA + MLA_KV_LORA:]
    rc, rs1, rs2 = rc_ref[...], rs1_ref[...], rs2_ref[...]
    q = _dot(_rms(cq, qg_ref[...]).astype(BF16), wuq_ref[...])
    for hd in range(MLA_HEADS):
        sl = slice(hd * HEAD_PAD, (hd + 1) * HEAD_PAD)
        mq_ref[:, sl] = (_rope(q[:, sl], rc, rs1, rs2) * Q_SCALE).astype(BF16)
    ckvn = _rms(ckv, kvg_ref[...]).astype(BF16)
    kn = _dot(ckvn, wuk_ref[...])
    krr = _rope(kr, rc, rs1, rs2)
    for hd in range(MLA_HEADS):
        sl = slice(hd * HEAD_PAD, (hd + 1) * HEAD_PAD)
        mk_ref[:, sl] = (kn[:, sl] + krr).astype(BF16)
    mv_ref[...] = (_dot(ckvn, wuv_ref[...]) + vone_ref[...]).astype(BF16)


def _inproj(x, sh, sc, g, lw, rope_tabs, tm):
    b, t, d = x.shape
    nblk = t // 8
    tok = lambda n: pl.BlockSpec((None, tm, n), lambda i, j: (i, j, 0))
    halo_before = pl.BlockSpec((None, 8, d), lambda i, j: (i, jnp.maximum(j * (tm // 8) - 1, 0), 0))
    halo_after = pl.BlockSpec((None, 8, d), lambda i, j: (i, jnp.minimum((j + 1) * (tm // 8), nblk - 1), 0))
    mod = pl.BlockSpec((None, 1, d), lambda i, j: (i, 0, 0))
    tab = pl.BlockSpec((tm, LANE), lambda i, j: (j, 0))
    outs = [(HY_WIDTH, F32)] * 3 + [(RET_W, BF16)] * 3 + [(RET_W, F32)] + [(MLA_QW, BF16)] * 3
    vone = jnp.zeros((MLA_HEADS, HEAD_PAD), F32).at[:, MLA_V].set(1.0).reshape(1, MLA_QW)
    return pl.pallas_call(
        _inproj_kernel,
        grid=(b, t // tm),
        in_specs=[tok(d), halo_before, halo_after, mod, mod, _full((1, d)), _full(lw["w_in"].shape),
                  _full(lw["conv_w"].shape), _full(lw["conv_b"].shape), _full((1, MLA_Q_LORA)),
                  _full((1, MLA_KV_LORA)), _full(lw["w_uq"].shape), _full(lw["w_uk"].shape),
                  _full(lw["w_uv"].shape), _full((1, MLA_QW)), tab, tab, tab],
        out_specs=[tok(n) for n, _ in outs],
        out_shape=[jax.ShapeDtypeStruct((b, t, n), dt) for n, dt in outs],
        compiler_params=_params(("arbitrary", "arbitrary"), VMEM_LIMIT),
        name="inproj",
    )(x, x, x, sh, sc, g, lw["w_in"], lw["conv_w"], lw["conv_b"], lw["q_g"], lw["kv_g"], lw["w_uq"], lw["w_uk"],
      lw["w_uv"], vone, *rope_tabs)


def _filt_kernel(z_ref, w1_ref, b1_ref, sf_ref, w2_ref, b2_ref, w3a_ref, w3b_ref, dl_ref, k_ref, nrm_ref, *, rows):
    i = pl.program_id(0)
    hp = lax.Precision.HIGHEST
    z = z_ref[...]
    h = jnp.sin(sf_ref[0:1, :] * (jnp.dot(z, w1_ref[...], precision=hp, preferred_element_type=F32) + b1_ref[...]))
    h = jnp.sin(sf_ref[1:2, :] * (jnp.dot(h, w2_ref[...], precision=hp, preferred_element_type=F32) + b2_ref[...]))
    ha = jnp.dot(h, w3a_ref[...], precision=hp, preferred_element_type=F32)
    hb = jnp.dot(h, w3b_ref[...], precision=hp, preferred_element_type=F32)
    half = HY_ORDER * HY_WIDTH
    win_a = jnp.exp(-z[:, 0:1] * dl_ref[...])
    win_b = jnp.exp(-z[:, LANE:LANE + 1] * dl_ref[...])
    first = (i * rows + lax.broadcasted_iota(jnp.int32, (rows, half), 0)) == 0
    fwd = ha[:, :half] * win_a
    top = jnp.where(first, fwd + ha[:, half:] * win_a, fwd)
    bot = jnp.where(first, 0.0, hb * win_b)
    k_ref[0] = top
    k_ref[1] = bot

    @pl.when(i == 0)
    def _():
        nrm_ref[...] = jnp.zeros_like(nrm_ref)

    nrm_ref[...] += jnp.sum(jnp.abs(top), axis=0, keepdims=True) + jnp.sum(jnp.abs(bot), axis=0, keepdims=True)


def _hyena_filter(seq, fw):
    t = np.linspace(0.0, 1.0, seq)[:, None]
    bands = np.linspace(1e-4, HY_BANDS - 1, HY_BANDS)[None, :]
    ang = (2.0 * math.pi / seq) * np.arange(seq)[:, None] * bands
    z = np.concatenate([t, np.cos(ang), -np.sin(ang)], axis=-1)
    z2 = np.zeros((seq, 2 * LANE), np.float32)
    z2[:, :HY_EMB] = z
    z2[1:, LANE:LANE + HY_EMB] = z[1:][::-1]
    z2 = jnp.asarray(z2)
    max_decay = math.log(HY_TARGET) / HY_FAST_PCT
    min_decay = math.log(HY_TARGET) / HY_SLOW_PCT
    deltas = jnp.abs(jnp.linspace(min_decay, max_decay, HY_WIDTH, dtype=F32))
    dl = jnp.tile(deltas, HY_ORDER)[None, :]
    half = HY_ORDER * HY_WIDTH
    rows = min(seq, 512)
    k, nrm = pl.pallas_call(
        functools.partial(_filt_kernel, rows=rows),
        grid=(seq // rows,),
        in_specs=[pl.BlockSpec((rows, 2 * LANE), lambda i: (i, 0)), _full((2 * LANE, LANE)), _full((1, LANE)),
                  _full((2, LANE)), _full((LANE, LANE)), _full((1, LANE)), _full((LANE, 2 * half)),
                  _full((LANE, half)), _full((1, half))],
        out_specs=[pl.BlockSpec((2, rows, half), lambda i: (0, i, 0)), pl.BlockSpec((1, half), lambda i: (0, 0))],
        out_shape=[jax.ShapeDtypeStruct((2, seq, half), F32), jax.ShapeDtypeStruct((1, half), F32)],
        compiler_params=_params(("arbitrary",), VMEM_LIMIT),
        name="hyena_filter",
    )(z2, fw["w1"], fw["b1"], fw["sf"], fw["w2"], fw["b2"], fw["w3a"], fw["w3b"], dl)
    return k.reshape(2 * seq, half), nrm


def _dft_consts(n_a):
    idx = np.arange(DFT)
    ang = 2.0 * np.pi * ((idx[:, None] * idx[None, :]) % DFT) / DFT
    fr, fi = jnp.asarray(np.cos(ang), F32), jnp.asarray(-np.sin(ang), F32)
    tang = 2.0 * np.pi * (idx[:, None] * idx[None, :]) / (DFT * DFT)
    tr, ti = jnp.asarray(np.cos(tang), F32), jnp.asarray(-np.sin(tang), F32)
    c = tr[:, :, None] * fr[None, :, :n_a] - ti[:, :, None] * fi[None, :, :n_a]
    sn = tr[:, :, None] * fi[None, :, :n_a] + ti[:, :, None] * fr[None, :, :n_a]
    f1 = jnp.concatenate([c, sn], axis=1).astype(BF16)
    f3 = jnp.concatenate([jnp.swapaxes(c, 1, 2), jnp.swapaxes(sn, 1, 2)], axis=1).astype(BF16)
    return jnp.concatenate([fr, fi], axis=0).astype(BF16), f1, f3


def _cmul(ar, ai, br, bi):
    return ar * br - ai * bi, ar * bi + ai * br


def _fft1_kernel(x_ref, f_ref, or_ref, oi_ref, *, n_b, cw, has_im):
    if has_im:
        xr = jnp.swapaxes(x_ref[0], 0, 1)
        xi = jnp.swapaxes(x_ref[1], 0, 1)
    else:
        xr = jnp.swapaxes(x_ref[...], 0, 1)
    outs_r, outs_i = [], []
    for j in range(n_b):
        if has_im:
            o = _dot(f_ref[j], jnp.concatenate([xr[j], xi[j]], axis=1).astype(BF16))
            ar = o[:DFT, :cw] - o[DFT:, cw:]
            ai = o[:DFT, cw:] + o[DFT:, :cw]
        else:
            o = _dot(f_ref[j], xr[j].astype(BF16))
            ar, ai = o[:DFT], o[DFT:]
        outs_r.append(ar.astype(BF16))
        outs_i.append(ai.astype(BF16))
    or_ref[...] = jnp.swapaxes(jnp.stack(outs_r, axis=0), 0, 1)
    oi_ref[...] = jnp.swapaxes(jnp.stack(outs_i, axis=0), 0, 1)


def _fft1(x, f1, cw, has_im, n_b=16):
    g = x.shape[0]
    n_a = f1.shape[2]
    if has_im:
        xspec = pl.BlockSpec((None, 2, n_a, n_b, cw), lambda j, i: (i, 0, 0, j, 0))
    else:
        xspec = pl.BlockSpec((None, n_a, n_b, cw), lambda j, i: (i, 0, j, 0))
    ospec = pl.BlockSpec((None, DFT, n_b, cw), lambda j, i: (i, 0, j, 0))
    oshape = jax.ShapeDtypeStruct((g, DFT, DFT, cw), BF16)
    return pl.pallas_call(
        functools.partial(_fft1_kernel, n_b=n_b, cw=cw, has_im=has_im),
        grid=(DFT // n_b, g),
        in_specs=[xspec, pl.BlockSpec((n_b, 2 * DFT, n_a), lambda j, i: (j, 0, 0))],
        out_specs=[ospec, ospec],
        out_shape=[oshape, oshape],
        compiler_params=_params(("arbitrary", "arbitrary"), VMEM_LIMIT),
        name="fft_stage1",
    )(x, f1)


def _fft2_filter_kernel(ar_ref, ai_ref, f_ref, nrm_ref, kr_ref, ki_ref, *, qb, cw):
    f = f_ref[...]
    scale = 1.0 / (nrm_ref[...] * float(DFT * DFT))
    for qi in range(qb):
        o = _dot(f, jnp.concatenate([ar_ref[qi], ai_ref[qi]], axis=1))
        kr_ref[qi] = (o[:DFT, :cw] - o[DFT:, cw:]) * scale
        ki_ref[qi] = (o[:DFT, cw:] + o[DFT:, :cw]) * scale


def _fft2_filter(ar, ai, fcat, nrm, cw, qb=4):
    spec = pl.BlockSpec((qb, DFT, cw), lambda i: (i, 0, 0))
    oshape = jax.ShapeDtypeStruct((DFT, DFT, cw), F32)
    return pl.pallas_call(
        functools.partial(_fft2_filter_kernel, qb=qb, cw=cw),
        grid=(DFT // qb,),
        in_specs=[spec, spec, _full(fcat.shape), _full((1, cw))],
        out_specs=[spec, spec],
        out_shape=[oshape, oshape],
        compiler_params=_params(("arbitrary",), VMEM_LIMIT),
        name="fft_filter_stage2",
    )(ar.reshape(DFT, DFT, cw), ai.reshape(DFT, DFT, cw), fcat, nrm)


def _fft2_kernel(ar_ref, ai_ref, f_ref, kr_ref, ki_ref, br_ref, bi_ref, *, qb, groups, cw):
    f = f_ref[...]
    for qi in range(qb):
        parts = []
        for g in range(groups):
            parts += [ar_ref[g, qi], ai_ref[g, qi]]
        o = _dot(f, jnp.concatenate(parts, axis=1))
        kr, ki = kr_ref[qi], ki_ref[qi]
        ys = []
        for g in range(groups):
            rr = slice(2 * g * cw, (2 * g + 1) * cw)
            ii = slice((2 * g + 1) * cw, (2 * g + 2) * cw)
            xr = o[:DFT, rr] - o[DFT:, ii]
            xi = o[:DFT, ii] + o[DFT:, rr]
            yr, yi = _cmul(xr, xi, kr, ki)
            ys += [yr.astype(BF16), yi.astype(BF16)]
        o2 = _dot(f, jnp.concatenate(ys, axis=1))
        for g in range(groups):
            rr = slice(2 * g * cw, (2 * g + 1) * cw)
            ii = slice((2 * g + 1) * cw, (2 * g + 2) * cw)
            br_ref[g, qi] = (o2[:DFT, rr] + o2[DFT:, ii]).astype(BF16)
            bi_ref[g, qi] = (o2[:DFT, ii] - o2[DFT:, rr]).astype(BF16)


def _fft2(ar, ai, fcat, kr, ki, order, cw, qb=8):
    g = ar.shape[0]
    a4 = pl.BlockSpec((g, qb, DFT, cw), lambda i: (0, i, 0, 0))
    ksp = pl.BlockSpec((qb, DFT, cw), lambda i: (i, 0, order))
    oshape = jax.ShapeDtypeStruct((g, DFT, DFT, cw), BF16)
    return pl.pallas_call(
        functools.partial(_fft2_kernel, qb=qb, groups=g, cw=cw),
        grid=(DFT // qb,),
        in_specs=[a4, a4, _full(fcat.shape), ksp, ksp],
        out_specs=[a4, a4],
        out_shape=[oshape, oshape],
        compiler_params=_params(("arbitrary",), VMEM_LIMIT),
        name="fft_stage2",
    )(ar, ai, fcat, kr, ki)


def _fft3_kernel(br_ref, bi_ref, f_ref, z_ref, gate_ref, bias_ref, o_ref, *, n_b, cw, n_a):
    br = jnp.swapaxes(br_ref[...], 0, 1)
    bi = jnp.swapaxes(bi_ref[...], 0, 1)
    y0, y1 = [], []
    for j in range(n_b):
        o = _dot(f_ref[j], jnp.concatenate([br[j], bi[j]], axis=1))
        y0.append(o[:n_a, :cw] + o[n_a:, cw:])
        y1.append(o[:n_a, cw:] - o[n_a:, :cw])
    bias = bias_ref[...]
    for s, ys in enumerate((y0, y1)):
        y = jnp.swapaxes(jnp.stack(ys, axis=0), 0, 1)
        o_ref[s] = gate_ref[s] * (y + z_ref[s] * bias)


def _fft3(br, bi, f3, z, gate, bias, cw, n_b=16):
    g = br.shape[0]
    n_a = f3.shape[1] // 2
    bsp = pl.BlockSpec((None, DFT, n_b, cw), lambda j, i: (i, 0, j, 0))
    zsp = pl.BlockSpec((None, 2, n_a, n_b, cw), lambda j, i: (i, 0, 0, j, 0))
    return pl.pallas_call(
        functools.partial(_fft3_kernel, n_b=n_b, cw=cw, n_a=n_a),
        grid=(DFT // n_b, g),
        in_specs=[bsp, bsp, pl.BlockSpec((n_b, 2 * n_a, DFT), lambda j, i: (j, 0, 0)), zsp, zsp, _full((1, cw))],
        out_specs=zsp,
        out_shape=jax.ShapeDtypeStruct(z.shape, F32),
        compiler_params=_params(("arbitrary", "arbitrary"), VMEM_LIMIT),
        name="fft_stage3",
    )(br, bi, f3, z, gate, bias)


def _dft_tables(seq):
    assert 2 * seq == DFT * DFT
    fcat, f1_full, _ = _dft_consts(DFT)
    _, f1, f3 = _dft_consts(seq // DFT)
    return fcat, f1_full, f1, f3


def _hyena_long(x1, x2, v, fw, bias, tables):
    b, seq, w = v.shape
    assert b % 2 == 0
    n_a = seq // DFT
    fcat, f1_full, f1, f3 = tables
    kf, nrm = _hyena_filter(seq, fw)
    half = HY_ORDER * w
    far, fai = _fft1(kf.reshape(1, DFT, DFT, half), f1_full, half, False)
    kr, ki = _fft2_filter(far, fai, fcat, nrm, half)
    view = lambda a: a.reshape(b // 2, 2, n_a, DFT, w)
    z, gates = view(v), (view(x1), view(x2))
    for o in range(HY_ORDER):
        ar, ai = _fft1(z, f1, w, True)
        br, bi = _fft2(ar, ai, fcat, kr, ki, o, w)
        z = _fft3(br, bi, f3, z, gates[o], bias[o:o + 1], w)
    return z.reshape(b, seq, w)


def _ctx_hyena_kernel(x1_ref, x2_ref, v_ref, kf_ref, nrm_ref, fd_ref, fi_ref, bias_ref, o_ref, *, seq, w):
    n = 2 * seq
    fd = fd_ref[...]
    kc = _dot(fd, kf_ref[...].astype(BF16))
    scale = 1.0 / (nrm_ref[...] * float(n))
    kr, ki = kc[:n] * scale, kc[n:] * scale
    finv = fi_ref[...]
    z = v_ref[...]
    gates = (x1_ref[...], x2_ref[...])
    for o in range(HY_ORDER):
        sl = slice(o * w, (o + 1) * w)
        xf = _dot(fd[:, :seq], z.astype(BF16))
        yr, yi = _cmul(xf[:n], xf[n:], kr[:, sl], ki[:, sl])
        y = _dot(finv, jnp.concatenate([yr, yi], axis=0).astype(BF16))
        z = gates[o] * (y + z * bias_ref[o:o + 1, :])
    o_ref[...] = z


def _hyena_ctx(x1, x2, v, fw, bias):
    b, seq, w = v.shape
    n = 2 * seq
    idx = np.arange(n)
    ang = 2.0 * np.pi * ((idx[:, None] * idx[None, :]) % n) / n
    fr, fi = np.cos(ang), -np.sin(ang)
    fd = jnp.asarray(np.concatenate([fr, fi], axis=0), F32).astype(BF16)
    finv = jnp.asarray(np.concatenate([fr[:seq], fi[:seq]], axis=1), F32).astype(BF16)
    kf, nrm = _hyena_filter(seq, fw)
    tok = pl.BlockSpec((None, seq, w), lambda i: (i, 0, 0))
    return pl.pallas_call(
        functools.partial(_ctx_hyena_kernel, seq=seq, w=w),
        grid=(b,),
        in_specs=[tok, tok, tok, _full(kf.shape), _full(nrm.shape), _full(fd.shape), _full(finv.shape),
                  _full(bias.shape)],
        out_specs=tok,
        out_shape=jax.ShapeDtypeStruct(v.shape, F32),
        compiler_params=_params(("arbitrary",), VMEM_LIMIT),
        name="hyena_ctx",
    )(x1, x2, v, kf, nrm, fd, finv, bias)


def _ret_kernel(*refs, reverse, finalize, n_chunk):
    if finalize:
        q_ref, k_ref, v_ref, ld_ref, s0_ref, of_ref, g_ref, avg_ref, out_ref, sfin_ref, s_scr = refs
    else:
        q_ref, k_ref, v_ref, ld_ref, s0_ref, out_ref, sfin_ref, s_scr = refs
    c = q_ref.shape[0] // n_chunk
    t = pl.program_id(1)

    @pl.when(t == 0)
    def _():
        s_scr[...] = s0_ref[...]

    r = lax.broadcasted_iota(jnp.int32, (c, c), 0)
    m = lax.broadcasted_iota(jnp.int32, (c, c), 1)
    diff = ((m - r) if reverse else (r - m)).astype(F32)
    pos = lax.broadcasted_iota(jnp.int32, (c, RET_DV), 0).astype(F32)
    order = range(n_chunk - 1, -1, -1) if reverse else range(n_chunk)
    for hd in range(RET_HEADS):
        lg = jnp.log1p(-jnp.exp(ld_ref[hd]))
        lgv = lg[:, :RET_DV]
        lgc = jnp.concatenate([lg] * (c // LANE), axis=1)
        decay = jnp.where(diff >= 0, jnp.exp(lgc * jnp.maximum(diff, 0.0)), 0.0)
        if reverse:
            zeta = jnp.exp(lgv * pos)
            xi = jnp.exp(lgv * (c - pos))
        else:
            zeta = jnp.exp(lgv * (c - 1 - pos))
            xi = jnp.exp(lgv * (pos + 1))
        g_chunk = jnp.exp(lgv * c)
        hs = slice(hd * RET_DK, (hd + 1) * RET_DK)
        state = s_scr[hd]
        entering = {}
        for ci in order:
            rows = slice(ci * c, (ci + 1) * c)
            kv = lax.dot_general(k_ref[rows, hs].astype(BF16), (v_ref[rows, hs] * zeta).astype(BF16),
                                 (((0,), (0,)), ((), ())), preferred_element_type=F32)
            entering[ci] = state
            state = g_chunk * state + kv
        s_scr[hd] = state
        for ci in order:
            rows = slice(ci * c, (ci + 1) * c)
            qh = q_ref[rows, hs].astype(BF16)
            kh = k_ref[rows, hs].astype(BF16)
            s = lax.dot_general(qh, kh, (((1,), (1,)), ((), ())), preferred_element_type=F32)
            inner = _dot((s * decay).astype(BF16), v_ref[rows, hs].astype(BF16))
            o = inner + _dot(qh, entering[ci].astype(BF16)) * xi
            out_ref[rows, hs] = o

    if finalize:
        o = out_ref[...] + of_ref[...]
        ms = _dot((o * o).astype(BF16), avg_ref[...])
        out_ref[...] = jax.nn.silu(g_ref[...]) * (o * lax.rsqrt(ms + EPS))

    @pl.when(t == pl.num_programs(1) - 1)
    def _():
        sfin_ref[...] = s_scr[...]


def _ret_sweep(q, k, v, ld, s0, reverse, fwd_out=None, gate=None):
    b, t, w = q.shape
    tt = min(t, 1024)
    nt = t // tt
    finalize = fwd_out is not None
    tmap = (lambda i, j: (i, nt - 1 - j, 0)) if reverse else (lambda i, j: (i, j, 0))
    tok = pl.BlockSpec((None, tt, w), tmap)
    st = pl.BlockSpec((None, RET_HEADS, RET_DK, RET_DV), lambda i, j: (i, 0, 0, 0))
    ins = [q, k, v, ld, s0]
    in_specs = [tok, tok, tok, _full(ld.shape), st]
    if finalize:
        avg = jnp.kron(jnp.eye(RET_HEADS, dtype=F32), jnp.full((RET_DV, RET_DV), 1.0 / RET_DV, F32)).astype(BF16)
        ins += [fwd_out, gate, avg]
        in_specs += [tok, tok, _full(avg.shape)]
    return pl.pallas_call(
        functools.partial(_ret_kernel, reverse=reverse, finalize=finalize, n_chunk=tt // RET_KCHUNK),
        grid=(b, nt),
        in_specs=in_specs,
        out_specs=[tok, st],
        out_shape=[jax.ShapeDtypeStruct((b, t, w), F32),
                   jax.ShapeDtypeStruct((b, RET_HEADS, RET_DK, RET_DV), F32)],
        scratch_shapes=[pltpu.VMEM((RET_HEADS, RET_DK, RET_DV), F32)],
        compiler_params=_params(("arbitrary", "arbitrary"), VMEM_LIMIT),
        name="retention_bwd" if reverse else "retention_fwd",
    )(*ins)


def _retention(q, k, v, g, ld, s_f, s_b):
    out_f, fin_f = _ret_sweep(q, k, v, ld[0], s_f, False)
    out, fin_b = _ret_sweep(q, k, v, ld[1], s_b, True, out_f, g)
    return out, fin_f, fin_b


def _attn_kernel(*refs, n_src, tk):
    q_ref = refs[0]
    o_ref, s_buf, p_buf = refs[1 + 2 * n_src:]
    tq = q_ref.shape[0]
    chunks = []
    for si in range(n_src):
        k_ref, v_ref = refs[1 + 2 * si], refs[2 + 2 * si]
        rows = min(tk, k_ref.shape[0])
        chunks += [(k_ref, v_ref, r0, rows) for r0 in range(0, k_ref.shape[0], rows)]
    n = len(chunks)
    for hh in range(2):
        hs = slice(hh * HEAD_PAD, (hh + 1) * HEAD_PAD)
        q = q_ref[:, hs]

        def scores(c, slot):
            k_ref, _, r0, rows = chunks[c]
            s_buf[slot, :, :rows] = lax.dot_general(q, k_ref[r0:r0 + rows, hs], (((1,), (1,)), ((), ())),
                                                    preferred_element_type=F32)

        def pv(c, slot):
            _, v_ref, r0, rows = chunks[c]
            return _dot(p_buf[slot, :, :rows], v_ref[r0:r0 + rows, hs])

        scores(0, 0)
        m = jnp.full((tq, 1), -jnp.inf, F32)
        acc = jnp.zeros((tq, HEAD_PAD), F32)
        a_prev = None
        for c in range(n):
            slot = c % 2
            if c + 1 < n:
                scores(c + 1, 1 - slot)
            if c >= 1:
                acc = a_prev * acc + pv(c - 1, 1 - slot)
            rows = chunks[c][3]
            s = s_buf[slot, :, :rows]
            m_new = jnp.maximum(m, jnp.max(s, axis=-1, keepdims=True))
            a_prev = jnp.exp2(m - m_new)
            p_buf[slot, :, :rows] = jnp.exp2(s - m_new).astype(BF16)
            m = m_new
        acc = a_prev * acc + pv(n - 1, (n - 1) % 2)
        o_ref[:, hh * MLA_V:(hh + 1) * MLA_V] = (acc[:, :MLA_V] / acc[:, MLA_V:MLA_V + 1]).astype(o_ref.dtype)


def _attention(q, kvs, tq, tk):
    b, t, _ = q.shape
    in_specs = [pl.BlockSpec((None, tq, 2 * HEAD_PAD), lambda i, h, j: (i, j, h))]
    args = [q]
    for k, v in kvs:
        in_specs += [pl.BlockSpec((None, k.shape[1], 2 * HEAD_PAD), lambda i, h, j: (i, 0, h))] * 2
        args += [k, v]
    tk = min(tk, max(k.shape[1] for k, _ in kvs))
    return pl.pallas_call(
        functools.partial(_attn_kernel, n_src=len(kvs), tk=tk),
        grid=(b, MLA_HEADS // 2, t // tq),
        in_specs=in_specs,
        out_specs=pl.BlockSpec((None, tq, 2 * MLA_V), lambda i, h, j: (i, j, h)),
        out_shape=jax.ShapeDtypeStruct((b, t, MLA_VW), BF16),
        scratch_shapes=[pltpu.VMEM((2, tq, tk), F32), pltpu.VMEM((2, tq, tk), BF16)],
        compiler_params=_params(("arbitrary", "arbitrary", "arbitrary"), VMEM_LIMIT),
        name="mla_attention",
    )(*args)


def _outmlp_kernel(*refs, final, ffc):
    if final:
        (x_ref, hy_ref, ret_ref, att_ref, ga1_ref, sh2_ref, sc2_ref, ga2_ref, g2_ref, wo_ref, w1_ref, w2_ref,
         fg_ref, o_ref) = refs
    else:
        (x_ref, hy_ref, ret_ref, att_ref, ga1_ref, sh2_ref, sc2_ref, ga2_ref, g2_ref, wo_ref, w1_ref, w2_ref,
         o_ref) = refs
    w = HY_WIDTH
    mix = (_dot(hy_ref[...].astype(BF16), wo_ref[0:w, :]) + _dot(ret_ref[...].astype(BF16), wo_ref[w:2 * w, :])
           + _dot(att_ref[...], wo_ref[2 * w:, :]))
    x = x_ref[...] + ga1_ref[...] * mix
    h = (_rms(x, g2_ref[...]) * (1 + sc2_ref[...]) + sh2_ref[...]).astype(BF16)
    acc = jnp.zeros(x.shape, F32)
    for c0 in range(0, D_FF, ffc):
        hid = jnp.square(jax.nn.relu(_dot(h, w1_ref[:, c0:c0 + ffc]))).astype(BF16)
        acc = acc + _dot(hid, w2_ref[c0:c0 + ffc, :])
    x = x + ga2_ref[...] * acc
    if final:
        x = _rms(x, fg_ref[...])
    o_ref[...] = x


def _outmlp(x, hy, ret, att, ga1, sh2, sc2, ga2, lw, final_g, tm):
    b, t, d = x.shape
    tok = lambda n: pl.BlockSpec((None, tm, n), lambda i, j: (i, j, 0))
    mod = pl.BlockSpec((None, 1, d), lambda i, j: (i, 0, 0))
    final = final_g is not None
    ins = [x, hy, ret, att, ga1, sh2, sc2, ga2, lw["norm2_g"], lw["w_out"], lw["w1"], lw["w2"]]
    in_specs = [tok(d), tok(hy.shape[-1]), tok(ret.shape[-1]), tok(att.shape[-1]), mod, mod, mod, mod,
                _full((1, d)), _full(lw["w_out"].shape), _full(lw["w1"].shape), _full(lw["w2"].shape)]
    if final:
        ins.append(final_g)
        in_specs.append(_full((1, d)))
    return pl.pallas_call(
        functools.partial(_outmlp_kernel, final=final, ffc=1024),
        grid=(b, t // tm),
        in_specs=in_specs,
        out_specs=tok(d),
        out_shape=jax.ShapeDtypeStruct((b, t, d), F32),
        compiler_params=_params(("arbitrary", "arbitrary"), VMEM_LIMIT),
        name="outproj_mlp",
    )(*ins)


def _rope_tables(t):
    rows = t // GRID_W
    row = jnp.repeat(jnp.arange(rows), GRID_W).astype(F32)
    col = jnp.tile(jnp.arange(GRID_W), rows).astype(F32)
    n_freq = MLA_ROPE // 4
    inv = ROPE_BASE ** (-jnp.arange(n_freq, dtype=F32) / n_freq)
    ang_r = row[:, None] * inv
    ang_c = col[:, None] * inv
    cr, sr, cc, sc = jnp.cos(ang_r), jnp.sin(ang_r), jnp.cos(ang_c), jnp.sin(ang_c)
    one = jnp.ones((t, MLA_NOPE), F32)
    z64 = jnp.zeros((t, MLA_NOPE), F32)
    z8 = jnp.zeros((t, n_freq), F32)
    tail1 = jnp.ones((t, HEAD_PAD - MLA_QK), F32)
    tail0 = jnp.zeros((t, HEAD_PAD - MLA_QK), F32)
    c = jnp.concatenate([one, cr, cr, cc, cc, tail1], axis=1)
    s1 = jnp.concatenate([z64, -sr, z8, -sc, z8, tail0], axis=1)
    s2 = jnp.concatenate([z64, z8, sr, z8, sc, tail0], axis=1)
    return c, s1, s2


def _identity_rope_tables(t):
    return jnp.ones((t, HEAD_PAD), F32), jnp.zeros((t, HEAD_PAD), F32), jnp.zeros((t, HEAD_PAD), F32)


def _layer_weights(p):
    d = D_MODEL
    w_in = p["w_in"]
    hy_cols = 3 * HY_WIDTH
    ret_cols = 4 * RET_W
    w_mla = w_in[:, hy_cols + ret_cols:]
    pad_l = jnp.zeros((d, MLA_NOPE), F32)
    pad_r = jnp.zeros((d, HEAD_PAD - MLA_QK), F32)
    w_mla = jnp.concatenate([w_mla[:, :MLA_Q_LORA + MLA_KV_LORA], pad_l, w_mla[:, MLA_Q_LORA + MLA_KV_LORA:], pad_r],
                            axis=1)
    w_in_pad = jnp.concatenate([w_in[:, :hy_cols + ret_cols], w_mla], axis=1).astype(BF16)
    w_uq = p["mla_w_uq"].reshape(MLA_Q_LORA, MLA_HEADS, MLA_QK)
    w_uq = jnp.pad(w_uq, ((0, 0), (0, 0), (0, HEAD_PAD - MLA_QK))).reshape(MLA_Q_LORA, MLA_QW).astype(BF16)
    w_ukv = p["mla_w_ukv"].reshape(MLA_KV_LORA, MLA_HEADS, MLA_NOPE + MLA_V)
    w_uk = jnp.pad(w_ukv[:, :, :MLA_NOPE], ((0, 0), (0, 0), (0, HEAD_PAD - MLA_NOPE)))
    w_uk = w_uk.reshape(MLA_KV_LORA, MLA_QW).astype(BF16)
    w_uv = jnp.pad(w_ukv[:, :, MLA_NOPE:], ((0, 0), (0, 0), (0, HEAD_PAD - MLA_V)))
    w_uv = w_uv.reshape(MLA_KV_LORA, MLA_QW).astype(BF16)
    f = HY_FFN
    half = HY_ORDER * HY_WIDTH
    two = lambda a: jnp.concatenate([a, a], axis=-1)
    w1 = jnp.zeros((2 * LANE, LANE), F32)
    w1 = w1.at[:HY_EMB, :f].set(p["hy_ffn_w1"]).at[LANE:LANE + HY_EMB, f:].set(p["hy_ffn_w1"])
    w2 = jnp.zeros((LANE, LANE), F32).at[:f, :f].set(p["hy_ffn_w2"]).at[f:, f:].set(p["hy_ffn_w2"])
    fw = {
        "w1": w1, "b1": two(p["hy_ffn_b1"])[None, :], "sf": two(p["hy_sin_freq"]),
        "w2": w2, "b2": two(p["hy_ffn_b2"])[None, :],
        "w3a": jnp.pad(p["hy_ffn_w3"], ((0, f), (0, 0))),
        "w3b": jnp.pad(p["hy_ffn_w3"][:, half:], ((f, 0), (0, 0))),
    }
    ld = jnp.broadcast_to(p["ret_log_decay"][:, :, None, None], (2, RET_HEADS, 1, LANE))
    return {
        "w_in": w_in_pad, "w_uq": w_uq, "w_uk": w_uk, "w_uv": w_uv,
        "q_g": p["mla_q_norm_g"][None, :], "kv_g": p["mla_kv_norm_g"][None, :],
        "norm1_g": p["norm1_g"][None, :], "norm2_g": p["norm2_g"][None, :],
        "w_out": p["w_out"].astype(BF16), "w1": p["mlp_w1"].astype(BF16), "w2": p["mlp_w2"].astype(BF16),
        "conv_w": p["hy_conv_w"], "conv_b": p["hy_conv_b"][None, :], "hy_bias": p["hy_bias"],
        "filt": fw, "ld": ld,
    }


def _mixers(x, sh1, sc1, lw, rope_tabs, tm):
    x1, x2, v, rq, rk, rv, rg, mq, mk, mv = _inproj(x, sh1, sc1, lw["norm1_g"], lw, rope_tabs, tm)
    return (x1, x2, v), (rq, rk, rv, rg), (mq, mk, mv)


def kernel(x, c, ctx, c_ctx, w_ada, b_ada, norm1_g, norm2_g, w_in, w_out, hy_conv_w, hy_conv_b, hy_ffn_w1,
           hy_ffn_b1, hy_sin_freq, hy_ffn_w2, hy_ffn_b2, hy_ffn_w3, hy_bias, ret_log_decay, mla_q_norm_g, mla_w_uq,
           mla_kv_norm_g, mla_w_ukv, mlp_w1, mlp_w2, final_norm_g):
    b, seq, d = x.shape
    n_ctx = ctx.shape[1]
    depth = w_ada.shape[0]
    cond8 = jnp.concatenate([c, c_ctx[None, :], jnp.zeros((8 - b - 1, d), F32)], axis=0)
    ada = _ada(cond8, w_ada, b_ada)
    rope_lat = _rope_tables(seq)
    rope_ctx = _identity_rope_tables(n_ctx)
    dft_tables = _dft_tables(seq)
    zero_state = jnp.zeros((b, RET_HEADS, RET_DK, RET_DV), F32)
    xc = ctx
    for i in range(depth):
        p = {
            "w_in": w_in[i], "w_out": w_out[i], "norm1_g": norm1_g[i], "norm2_g": norm2_g[i],
            "hy_conv_w": hy_conv_w[i], "hy_conv_b": hy_conv_b[i], "hy_ffn_w1": hy_ffn_w1[i],
            "hy_ffn_b1": hy_ffn_b1[i], "hy_sin_freq": hy_sin_freq[i], "hy_ffn_w2": hy_ffn_w2[i],
            "hy_ffn_b2": hy_ffn_b2[i], "hy_ffn_w3": hy_ffn_w3[i], "hy_bias": hy_bias[i],
            "ret_log_decay": ret_log_decay[i], "mla_q_norm_g": mla_q_norm_g[i], "mla_w_uq": mla_w_uq[i],
            "mla_kv_norm_g": mla_kv_norm_g[i], "mla_w_ukv": mla_w_ukv[i], "mlp_w1": mlp_w1[i], "mlp_w2": mlp_w2[i],
        }
        lw = _layer_weights(p)
        last = i == depth - 1
        terms = [ada[i, :, k * d:(k + 1) * d] for k in range(N_MOD)]
        lat = [tm[:b, None, :] for tm in terms]
        cx = [jnp.broadcast_to(tm[b:b + 1, None, :], (b, 1, d)) for tm in terms]

        hy_c, ret_c, (cq, ck, cv) = _mixers(xc, cx[0], cx[1], lw, rope_ctx, n_ctx)
        hy_l, ret_l, (mq, mk, mv) = _mixers(x, lat[0], lat[1], lw, rope_lat, 1024)

        rq, rk, rv, rg = ret_c
        ret_ctx, s_f, s_b = _retention(rq, rk, rv, rg, lw["ld"], zero_state, zero_state)
        rq, rk, rv, rg = ret_l
        ret, _, _ = _retention(rq, rk, rv, rg, lw["ld"], s_f, s_b)

        hy = _hyena_long(*hy_l, lw["filt"], lw["hy_bias"], dft_tables)

        att = _attention(mq, [(mk, mv), (ck, cv)], 1024, ATTN_TK)

        fg = final_norm_g[None, :] if last else None
        x = _outmlp(x, hy, ret, att, lat[2], lat[3], lat[4], lat[5], lw, fg, 512)
        if not last:
            hyc = _hyena_ctx(*hy_c, lw["filt"], lw["hy_bias"])
            att_c = _attention(cq, [(ck, cv)], n_ctx, ATTN_TK)
            xc = _outmlp(xc, hyc, ret_ctx, att_c, cx[2], cx[3], cx[4], cx[5], lw, None, n_ctx)
    return x
```

```python
import functools
import math

import numpy as np
import jax
import jax.numpy as jnp
from jax import lax
from jax.experimental import pallas as pl
from jax.experimental.pallas import tpu as pltpu

F32 = jnp.float32
BF16 = jnp.bfloat16

D_MODEL = 1024
DEPTH = 2
GRID_W = 64
D_FF = 4 * D_MODEL
N_MOD = 6
EPS = 1e-6

HY_WIDTH = D_MODEL // 4
HY_ORDER = 2
HY_EMB = 33
HY_BANDS = (HY_EMB - 1) // 2
HY_FFN = 64
HY_TARGET = 1e-2
HY_FAST_PCT = 0.3
HY_SLOW_PCT = 1.5

RET_DK = 64
RET_DV = 64
RET_HEADS = (D_MODEL // 4) // RET_DV
RET_KCHUNK = 256
RET_W = RET_HEADS * RET_DK

MLA_V = 64
MLA_HEADS = (D_MODEL // 2) // MLA_V
MLA_NOPE = 64
MLA_ROPE = 32
MLA_QK = MLA_NOPE + MLA_ROPE
MLA_Q_LORA = D_MODEL // 4
MLA_KV_LORA = D_MODEL // 8
ROPE_BASE = 10000.0

LANE = 128
HEAD_PAD = LANE
MLA_QW = MLA_HEADS * HEAD_PAD
MLA_VW = MLA_HEADS * MLA_V
MLA_IN_PAD = MLA_Q_LORA + MLA_KV_LORA + HEAD_PAD
IN_PAD = 3 * HY_WIDTH + 4 * RET_W + MLA_IN_PAD
Q_SCALE = MLA_QK ** -0.5 * math.log2(math.e)
ATTN_TK = 1024
DFT = 128
VMEM_LIMIT = 56 * 1024 * 1024


def _params(sem, vmem=None):
    return pltpu.CompilerParams(dimension_semantics=sem, vmem_limit_bytes=vmem)


def _full(shape):
    n = len(shape)
    return pl.BlockSpec(shape, lambda *_: (0,) * n, pipeline_mode=pl.Buffered(1))


def _dot(a, b):
    return jnp.dot(a, b, preferred_element_type=F32)


def _rms(x, g):
    return x * lax.rsqrt(jnp.mean(x * x, axis=-1, keepdims=True) + EPS) * g


def _ada_kernel(c_ref, w_ref, b_ref, o_ref):
    s = jax.nn.silu(c_ref[...])
    o_ref[...] = _dot(s.astype(BF16), w_ref[...].astype(BF16)) + b_ref[...]


def _ada(cond8, w_ada, b_ada):
    depth, d, n = w_ada.shape
    tn = 1536
    return pl.pallas_call(
        _ada_kernel,
        grid=(depth, n // tn),
        in_specs=[pl.BlockSpec((8, d), lambda l, j: (0, 0)),
                  pl.BlockSpec((None, d, tn), lambda l, j: (l, 0, j)),
                  pl.BlockSpec((None, 1, tn), lambda l, j: (l, 0, j))],
        out_specs=pl.BlockSpec((None, 8, tn), lambda l, j: (l, 0, j)),
        out_shape=jax.ShapeDtypeStruct((depth, 8, n), F32),
        compiler_params=_params(("arbitrary", "arbitrary"), VMEM_LIMIT),
        name="ada",
    )(cond8, w_ada, b_ada.reshape(depth, 1, n))


def _rope(t, c, s1, s2):
    return t * c + pltpu.roll(t, LANE - 8, 1) * s1 + pltpu.roll(t, 8, 1) * s2


def _inproj_kernel(x_ref, xp_ref, xn_ref, sh_ref, sc_ref, g_ref, win_ref, cw_ref, cb_ref, qg_ref, kvg_ref, wuq_ref,
                   wuk_ref, wuv_ref, vone_ref, rc_ref, rs1_ref, rs2_ref,
                   x1_ref, x2_ref, v_ref, rq_ref, rk_ref, rv_ref, rg_ref, mq_ref, mk_ref, mv_ref):
    w = HY_WIDTH
    j = pl.program_id(1)
    tm = x_ref.shape[0]
    mod = lambda t: (_rms(t, g_ref[...]) * (1 + sc_ref[...]) + sh_ref[...]).astype(BF16)
    h = mod(x_ref[...])
    edge = _dot(mod(jnp.concatenate([xp_ref[...], xn_ref[...]], axis=0)), win_ref[:, 0:3 * w])
    before = jnp.where(j > 0, edge[7:8], 0.0)
    after = jnp.where(j < pl.num_programs(1) - 1, edge[8:9], 0.0)
    first = lax.broadcasted_iota(jnp.int32, (tm, w), 0) == 0
    last = lax.broadcasted_iota(jnp.int32, (tm, w), 0) == tm - 1
    for gi, o_ref in enumerate((x1_ref, x2_ref, v_ref)):
        cs = slice(gi * w, (gi + 1) * w)
        u = _dot(h, win_ref[:, cs])
        up = jnp.where(first, before[:, cs], pltpu.roll(u, 1, 0))
        dn = jnp.where(last, after[:, cs], pltpu.roll(u, tm - 1, 0))
        o_ref[...] = cb_ref[:, cs] + up * cw_ref[0:1, cs] + u * cw_ref[1:2, cs] + dn * cw_ref[2:3, cs]
    o = 3 * w
    rq_ref[...] = _dot(h, win_ref[:, o:o + RET_W]).astype(BF16)
    rk_ref[...] = (_dot(h, win_ref[:, o + RET_W:o + 2 * RET_W]) * RET_DK ** -0.5).astype(BF16)
    rv_ref[...] = _dot(h, win_ref[:, o + 2 * RET_W:o + 3 * RET_W]).astype(BF16)
    rg_ref[...] = _dot(h, win_ref[:, o + 3 * RET_W:o + 4 * RET_W])
    o = o + 4 * RET_W
    um = _dot(h, win_ref[:, o:o + MLA_IN_PAD])
    cq = um[:, :MLA_Q_LORA]
    ckv = um[:, MLA_Q_LORA:MLA_Q_LORA + MLA_KV_LORA]
    kr = um[:, MLA_Q_LORA + MLA_KV_LORA:]
    rc, rs1, rs2 = rc_ref[...], rs1_ref[...], rs2_ref[...]
    q = _dot(_rms(cq, qg_ref[...]).astype(BF16), wuq_ref[...])
    for hd in range(MLA_HEADS):
        sl = slice(hd * HEAD_PAD, (hd + 1) * HEAD_PAD)
        mq_ref[:, sl] = (_rope(q[:, sl], rc, rs1, rs2) * Q_SCALE).astype(BF16)
    ckvn = _rms(ckv, kvg_ref[...]).astype(BF16)
    kn = _dot(ckvn, wuk_ref[...])
    krr = _rope(kr, rc, rs1, rs2)
    for hd in range(MLA_HEADS):
        sl = slice(hd * HEAD_PAD, (hd + 1) * HEAD_PAD)
        mk_ref[:, sl] = (kn[:, sl] + krr).astype(BF16)
    mv_ref[...] = (_dot(ckvn, wuv_ref[...]) + vone_ref[...]).astype(BF16)


def _inproj(x, sh, sc, g, lw, rope_tabs, tm):
    b, t, d = x.shape
    nblk = t // 8
    tok = lambda n: pl.BlockSpec((None, tm, n), lambda i, j: (i, j, 0))
    halo_before = pl.BlockSpec((None, 8, d), lambda i, j: (i, jnp.maximum(j * (tm // 8) - 1, 0), 0))
    halo_after = pl.BlockSpec((None, 8, d), lambda i, j: (i, jnp.minimum((j + 1) * (tm // 8), nblk - 1), 0))
    mod = pl.BlockSpec((None, 1, d), lambda i, j: (i, 0, 0))
    tab = pl.BlockSpec((tm, LANE), lambda i, j: (j, 0))
    outs = [(HY_WIDTH, F32)] * 3 + [(RET_W, BF16)] * 3 + [(RET_W, F32)] + [(MLA_QW, BF16)] * 3
    vone = jnp.zeros((MLA_HEADS, HEAD_PAD), F32).at[:, MLA_V].set(1.0).reshape(1, MLA_QW)
    return pl.pallas_call(
        _inproj_kernel,
        grid=(b, t // tm),
        in_specs=[tok(d), halo_before, halo_after, mod, mod, _full((1, d)), _full(lw["w_in"].shape),
                  _full(lw["conv_w"].shape), _full(lw["conv_b"].shape), _full((1, MLA_Q_LORA)),
                  _full((1, MLA_KV_LORA)), _full(lw["w_uq"].shape), _full(lw["w_uk"].shape),
                  _full(lw["w_uv"].shape), _full((1, MLA_QW)), tab, tab, tab],
        out_specs=[tok(n) for n, _ in outs],
        out_shape=[jax.ShapeDtypeStruct((b, t, n), dt) for n, dt in outs],
        compiler_params=_params(("arbitrary", "arbitrary"), VMEM_LIMIT),
        name="inproj",
    )(x, x, x, sh, sc, g, lw["w_in"], lw["conv_w"], lw["conv_b"], lw["q_g"], lw["kv_g"], lw["w_uq"], lw["w_uk"],
      lw["w_uv"], vone, *rope_tabs)


def _filt_kernel(z_ref, w1_ref, b1_ref, sf_ref, w2_ref, b2_ref, w3a_ref, w3b_ref, dl_ref, k_ref, nrm_ref, *, rows):
    i = pl.program_id(0)
    hp = lax.Precision.HIGHEST
    z = z_ref[...]
    h = jnp.sin(sf_ref[0:1, :] * (jnp.dot(z, w1_ref[...], precision=hp, preferred_element_type=F32) + b1_ref[...]))
    h = jnp.sin(sf_ref[1:2, :] * (jnp.dot(h, w2_ref[...], precision=hp, preferred_element_type=F32) + b2_ref[...]))
    ha = jnp.dot(h, w3a_ref[...], precision=hp, preferred_element_type=F32)
    hb = jnp.dot(h, w3b_ref[...], precision=hp, preferred_element_type=F32)
    half = HY_ORDER * HY_WIDTH
    win_a = jnp.exp(-z[:, 0:1] * dl_ref[...])
    win_b = jnp.exp(-z[:, LANE:LANE + 1] * dl_ref[...])
    first = (i * rows + lax.broadcasted_iota(jnp.int32, (rows, half), 0)) == 0
    fwd = ha[:, :half] * win_a
    top = jnp.where(first, fwd + ha[:, half:] * win_a, fwd)
    bot = jnp.where(first, 0.0, hb * win_b)
    k_ref[0] = top
    k_ref[1] = bot

    @pl.when(i == 0)
    def _():
        nrm_ref[...] = jnp.zeros_like(nrm_ref)

    nrm_ref[...] += jnp.sum(jnp.abs(top), axis=0, keepdims=True) + jnp.sum(jnp.abs(bot), axis=0, keepdims=True)


def _hyena_filter(seq, fw):
    t = np.linspace(0.0, 1.0, seq)[:, None]
    bands = np.linspace(1e-4, HY_BANDS - 1, HY_BANDS)[None, :]
    ang = (2.0 * math.pi / seq) * np.arange(seq)[:, None] * bands
    z = np.concatenate([t, np.cos(ang), -np.sin(ang)], axis=-1)
    z2 = np.zeros((seq, 2 * LANE), np.float32)
    z2[:, :HY_EMB] = z
    z2[1:, LANE:LANE + HY_EMB] = z[1:][::-1]
    z2 = jnp.asarray(z2)
    max_decay = math.log(HY_TARGET) / HY_FAST_PCT
    min_decay = math.log(HY_TARGET) / HY_SLOW_PCT
    deltas = jnp.abs(jnp.linspace(min_decay, max_decay, HY_WIDTH, dtype=F32))
    dl = jnp.tile(deltas, HY_ORDER)[None, :]
    half = HY_ORDER * HY_WIDTH
    rows = min(seq, 512)
    k, nrm = pl.pallas_call(
        functools.partial(_filt_kernel, rows=rows),
        grid=(seq // rows,),
        in_specs=[pl.BlockSpec((rows, 2 * LANE), lambda i: (i, 0)), _full((2 * LANE, LANE)), _full((1, LANE)),
                  _full((2, LANE)), _full((LANE, LANE)), _full((1, LANE)), _full((LANE, 2 * half)),
                  _full((LANE, half)), _full((1, half))],
        out_specs=[pl.BlockSpec((2, rows, half), lambda i: (0, i, 0)), pl.BlockSpec((1, half), lambda i: (0, 0))],
        out_shape=[jax.ShapeDtypeStruct((2, seq, half), F32), jax.ShapeDtypeStruct((1, half), F32)],
        compiler_params=_params(("arbitrary",), VMEM_LIMIT),
        name="hyena_filter",
    )(z2, fw["w1"], fw["b1"], fw["sf"], fw["w2"], fw["b2"], fw["w3a"], fw["w3b"], dl)
    return k.reshape(2 * seq, half), nrm


def _dft_tables(seq):
    assert 2 * seq == DFT * DFT
    n_a = seq // DFT
    idx = np.arange(DFT)
    ang = 2.0 * np.pi * ((idx[:, None] * idx[None, :]) % DFT) / DFT
    fr, fi = jnp.asarray(np.cos(ang), F32), jnp.asarray(-np.sin(ang), F32)
    tang = 2.0 * np.pi * (idx[:, None] * idx[None, :]) / (DFT * DFT)
    tr, ti = jnp.asarray(np.cos(tang), F32), jnp.asarray(-np.sin(tang), F32)
    c = tr[:, :, None] * fr[None] - ti[:, :, None] * fi[None]
    sn = tr[:, :, None] * fi[None] + ti[:, :, None] * fr[None]
    block = lambda p, q, r, s, ax: jnp.concatenate([jnp.concatenate([p, q], axis=ax + 1),
                                                    jnp.concatenate([r, s], axis=ax + 1)], axis=ax).astype(BF16)
    cd, sd = c[:, :, :n_a], sn[:, :, :n_a]
    ct, st = jnp.swapaxes(cd, 1, 2), jnp.swapaxes(sd, 1, 2)
    return {
        "fwd": block(fr, -fi, fi, fr, 0), "inv": block(fr, fi, -fi, fr, 0),
        "s1_real": jnp.concatenate([c, sn], axis=1).astype(BF16),
        "s1": block(cd, -sd, sd, cd, 1),
        "s3": block(ct, st, -st, ct, 1),
    }


def _cmul(ar, ai, br, bi):
    return ar * br - ai * bi, ar * bi + ai * br


def _fft1_kernel(x_ref, f_ref, or_ref, oi_ref, *, n_b, has_im):
    if has_im:
        x = jnp.concatenate([jnp.swapaxes(x_ref[0], 0, 1), jnp.swapaxes(x_ref[1], 0, 1)], axis=1)
    else:
        x = jnp.swapaxes(x_ref[...], 0, 1)
    outs_r, outs_i = [], []
    for j in range(n_b):
        o = _dot(f_ref[j], x[j].astype(BF16))
        outs_r.append(o[:DFT].astype(BF16))
        outs_i.append(o[DFT:].astype(BF16))
    or_ref[...] = jnp.swapaxes(jnp.stack(outs_r, axis=0), 0, 1)
    oi_ref[...] = jnp.swapaxes(jnp.stack(outs_i, axis=0), 0, 1)


def _fft1(x, f1, cw, has_im, n_b=16):
    g = x.shape[0]
    n_a = x.shape[-3]
    if has_im:
        xspec = pl.BlockSpec((None, 2, n_a, n_b, cw), lambda j, i: (i, 0, 0, j, 0))
    else:
        xspec = pl.BlockSpec((None, n_a, n_b, cw), lambda j, i: (i, 0, j, 0))
    ospec = pl.BlockSpec((None, DFT, n_b, cw), lambda j, i: (i, 0, j, 0))
    oshape = jax.ShapeDtypeStruct((g, DFT, DFT, cw), BF16)
    return pl.pallas_call(
        functools.partial(_fft1_kernel, n_b=n_b, has_im=has_im),
        grid=(DFT // n_b, g),
        in_specs=[xspec, pl.BlockSpec((n_b,) + f1.shape[1:], lambda j, i: (j, 0, 0))],
        out_specs=[ospec, ospec],
        out_shape=[oshape, oshape],
        compiler_params=_params(("arbitrary", "arbitrary"), VMEM_LIMIT),
        name="fft_stage1",
    )(x, f1)


def _fft2_filter_kernel(ar_ref, ai_ref, f_ref, nrm_ref, kr_ref, ki_ref, *, qb):
    f = f_ref[...]
    scale = 1.0 / (nrm_ref[...] * float(DFT * DFT))
    for qi in range(qb):
        o = _dot(f, jnp.concatenate([ar_ref[qi], ai_ref[qi]], axis=0))
        kr_ref[qi] = o[:DFT] * scale
        ki_ref[qi] = o[DFT:] * scale


def _fft2_filter(ar, ai, fwd, nrm, cw, qb=4):
    spec = pl.BlockSpec((qb, DFT, cw), lambda i: (i, 0, 0))
    oshape = jax.ShapeDtypeStruct((DFT, DFT, cw), F32)
    return pl.pallas_call(
        functools.partial(_fft2_filter_kernel, qb=qb),
        grid=(DFT // qb,),
        in_specs=[spec, spec, _full(fwd.shape), _full((1, cw))],
        out_specs=[spec, spec],
        out_shape=[oshape, oshape],
        compiler_params=_params(("arbitrary",), VMEM_LIMIT),
        name="fft_filter_stage2",
    )(ar.reshape(DFT, DFT, cw), ai.reshape(DFT, DFT, cw), fwd, nrm)


def _fft2_kernel(ar_ref, ai_ref, fwd_ref, inv_ref, kr_ref, ki_ref, br_ref, bi_ref, *, qb, groups, cw):
    fwd, inv = fwd_ref[...], inv_ref[...]
    for qi in range(qb):
        a = jnp.concatenate([jnp.concatenate([ar_ref[g, qi], ai_ref[g, qi]], axis=0) for g in range(groups)], axis=1)
        x = _dot(fwd, a)
        kr = jnp.concatenate([kr_ref[qi]] * groups, axis=1)
        ki = jnp.concatenate([ki_ref[qi]] * groups, axis=1)
        yr, yi = _cmul(x[:DFT], x[DFT:], kr, ki)
        o = _dot(inv, jnp.concatenate([yr, yi], axis=0).astype(BF16))
        for g in range(groups):
            br_ref[g, qi] = o[:DFT, g * cw:(g + 1) * cw].astype(BF16)
            bi_ref[g, qi] = o[DFT:, g * cw:(g + 1) * cw].astype(BF16)


def _fft2(ar, ai, fwd, inv, kr, ki, order, cw, qb=8):
    g = ar.shape[0]
    a4 = pl.BlockSpec((g, qb, DFT, cw), lambda i: (0, i, 0, 0))
    ksp = pl.BlockSpec((qb, DFT, cw), lambda i: (i, 0, order))
    oshape = jax.ShapeDtypeStruct((g, DFT, DFT, cw), BF16)
    return pl.pallas_call(
        functools.partial(_fft2_kernel, qb=qb, groups=g, cw=cw),
        grid=(DFT // qb,),
        in_specs=[a4, a4, _full(fwd.shape), _full(inv.shape), ksp, ksp],
        out_specs=[a4, a4],
        out_shape=[oshape, oshape],
        compiler_params=_params(("arbitrary",), VMEM_LIMIT),
        name="fft_stage2",
    )(ar, ai, fwd, inv, kr, ki)


def _fft3_kernel(br_ref, bi_ref, f_ref, z_ref, gate_ref, bias_ref, o_ref, *, n_b, n_a):
    b = jnp.concatenate([jnp.swapaxes(br_ref[...], 0, 1), jnp.swapaxes(bi_ref[...], 0, 1)], axis=1)
    y0, y1 = [], []
    for j in range(n_b):
        o = _dot(f_ref[j], b[j])
        y0.append(o[:n_a])
        y1.append(o[n_a:])
    bias = bias_ref[...]
    for s, ys in enumerate((y0, y1)):
        y = jnp.swapaxes(jnp.stack(ys, axis=0), 0, 1)
        o_ref[s] = gate_ref[s] * (y + z_ref[s] * bias)


def _fft3(br, bi, f3, z, gate, bias, cw, n_b=16):
    g = br.shape[0]
    n_a = f3.shape[1] // 2
    bsp = pl.BlockSpec((None, DFT, n_b, cw), lambda j, i: (i, 0, j, 0))
    zsp = pl.BlockSpec((None, 2, n_a, n_b, cw), lambda j, i: (i, 0, 0, j, 0))
    return pl.pallas_call(
        functools.partial(_fft3_kernel, n_b=n_b, n_a=n_a),
        grid=(DFT // n_b, g),
        in_specs=[bsp, bsp, pl.BlockSpec((n_b,) + f3.shape[1:], lambda j, i: (j, 0, 0)), zsp, zsp, _full((1, cw))],
        out_specs=zsp,
        out_shape=jax.ShapeDtypeStruct(z.shape, F32),
        compiler_params=_params(("arbitrary", "arbitrary"), VMEM_LIMIT),
        name="fft_stage3",
    )(br, bi, f3, z, gate, bias)


def _hyena_long(x1, x2, v, fw, bias, tables):
    b, seq, w = v.shape
    assert b % 2 == 0
    n_a = seq // DFT
    kf, nrm = _hyena_filter(seq, fw)
    half = HY_ORDER * w
    far, fai = _fft1(kf.reshape(1, DFT, DFT, half), tables["s1_real"], half, False)
    kr, ki = _fft2_filter(far, fai, tables["fwd"], nrm, half)
    view = lambda a: a.reshape(b // 2, 2, n_a, DFT, w)
    z, gates = view(v), (view(x1), view(x2))
    for o in range(HY_ORDER):
        ar, ai = _fft1(z, tables["s1"], w, True)
        br, bi = _fft2(ar, ai, tables["fwd"], tables["inv"], kr, ki, o, w)
        z = _fft3(br, bi, tables["s3"], z, gates[o], bias[o:o + 1], w)
    return z.reshape(b, seq, w)


def _ctx_hyena_kernel(x1_ref, x2_ref, v_ref, kf_ref, nrm_ref, fd_ref, fi_ref, bias_ref, o_ref, *, seq, w):
    n = 2 * seq
    fd = fd_ref[...]
    kc = _dot(fd, kf_ref[...].astype(BF16))
    scale = 1.0 / (nrm_ref[...] * float(n))
    kr, ki = kc[:n] * scale, kc[n:] * scale
    finv = fi_ref[...]
    z = v_ref[...]
    gates = (x1_ref[...], x2_ref[...])
    for o in range(HY_ORDER):
        sl = slice(o * w, (o + 1) * w)
        xf = _dot(fd[:, :seq], z.astype(BF16))
        yr, yi = _cmul(xf[:n], xf[n:], kr[:, sl], ki[:, sl])
        y = _dot(finv, jnp.concatenate([yr, yi], axis=0).astype(BF16))
        z = gates[o] * (y + z * bias_ref[o:o + 1, :])
    o_ref[...] = z


def _hyena_ctx(x1, x2, v, fw, bias):
    b, seq, w = v.shape
    n = 2 * seq
    idx = np.arange(n)
    ang = 2.0 * np.pi * ((idx[:, None] * idx[None, :]) % n) / n
    fr, fi = np.cos(ang), -np.sin(ang)
    fd = jnp.asarray(np.concatenate([fr, fi], axis=0), F32).astype(BF16)
    finv = jnp.asarray(np.concatenate([fr[:seq], fi[:seq]], axis=1), F32).astype(BF16)
    kf, nrm = _hyena_filter(seq, fw)
    tok = pl.BlockSpec((None, seq, w), lambda i: (i, 0, 0))
    return pl.pallas_call(
        functools.partial(_ctx_hyena_kernel, seq=seq, w=w),
        grid=(b,),
        in_specs=[tok, tok, tok, _full(kf.shape), _full(nrm.shape), _full(fd.shape), _full(finv.shape),
                  _full(bias.shape)],
        out_specs=tok,
        out_shape=jax.ShapeDtypeStruct(v.shape, F32),
        compiler_params=_params(("arbitrary",), VMEM_LIMIT),
        name="hyena_ctx",
    )(x1, x2, v, kf, nrm, fd, finv, bias)


def _ret_kernel(*refs, reverse, finalize, n_chunk):
    if finalize:
        q_ref, k_ref, v_ref, ld_ref, s0_ref, of_ref, g_ref, avg_ref, out_ref, sfin_ref, s_scr = refs
    else:
        q_ref, k_ref, v_ref, ld_ref, s0_ref, out_ref, sfin_ref, s_scr = refs
    c = q_ref.shape[0] // n_chunk
    t = pl.program_id(1)

    @pl.when(t == 0)
    def _():
        s_scr[...] = s0_ref[...]

    r = lax.broadcasted_iota(jnp.int32, (c, c), 0)
    m = lax.broadcasted_iota(jnp.int32, (c, c), 1)
    diff = ((m - r) if reverse else (r - m)).astype(F32)
    pos = lax.broadcasted_iota(jnp.int32, (c, RET_DV), 0).astype(F32)
    order = range(n_chunk - 1, -1, -1) if reverse else range(n_chunk)
    for hd in range(RET_HEADS):
        lg = jnp.log1p(-jnp.exp(ld_ref[hd]))
        lgv = lg[:, :RET_DV]
        lgc = jnp.concatenate([lg] * (c // LANE), axis=1)
        decay = jnp.where(diff >= 0, jnp.exp(lgc * jnp.maximum(diff, 0.0)), 0.0)
        if reverse:
            zeta = jnp.exp(lgv * pos)
            xi = jnp.exp(lgv * (c - pos))
        else:
            zeta = jnp.exp(lgv * (c - 1 - pos))
            xi = jnp.exp(lgv * (pos + 1))
        g_chunk = jnp.exp(lgv * c)
        hs = slice(hd * RET_DK, (hd + 1) * RET_DK)
        state = s_scr[hd]
        entering = {}
        for ci in order:
            rows = slice(ci * c, (ci + 1) * c)
            kv = lax.dot_general(k_ref[rows, hs].astype(BF16), (v_ref[rows, hs] * zeta).astype(BF16),
                                 (((0,), (0,)), ((), ())), preferred_element_type=F32)
            entering[ci] = state
            state = g_chunk * state + kv
        s_scr[hd] = state
        for ci in order:
            rows = slice(ci * c, (ci + 1) * c)
            qh = q_ref[rows, hs].astype(BF16)
            kh = k_ref[rows, hs].astype(BF16)
            s = lax.dot_general(qh, kh, (((1,), (1,)), ((), ())), preferred_element_type=F32)
            inner = _dot((s * decay).astype(BF16), v_ref[rows, hs].astype(BF16))
            o = inner + _dot(qh, entering[ci].astype(BF16)) * xi
            out_ref[rows, hs] = o

    if finalize:
        o = out_ref[...] + of_ref[...]
        ms = _dot((o * o).astype(BF16), avg_ref[...])
        out_ref[...] = jax.nn.silu(g_ref[...]) * (o * lax.rsqrt(ms + EPS))

    @pl.when(t == pl.num_programs(1) - 1)
    def _():
        sfin_ref[...] = s_scr[...]


def _ret_sweep(q, k, v, ld, s0, reverse, fwd_out=None, gate=None):
    b, t, w = q.shape
    tt = min(t, 1024)
    nt = t // tt
    finalize = fwd_out is not None
    tmap = (lambda i, j: (i, nt - 1 - j, 0)) if reverse else (lambda i, j: (i, j, 0))
    tok = pl.BlockSpec((None, tt, w), tmap)
    st = pl.BlockSpec((None, RET_HEADS, RET_DK, RET_DV), lambda i, j: (i, 0, 0, 0))
    ins = [q, k, v, ld, s0]
    in_specs = [tok, tok, tok, _full(ld.shape), st]
    if finalize:
        avg = jnp.kron(jnp.eye(RET_HEADS, dtype=F32), jnp.full((RET_DV, RET_DV), 1.0 / RET_DV, F32)).astype(BF16)
        ins += [fwd_out, gate, avg]
        in_specs += [tok, tok, _full(avg.shape)]
    return pl.pallas_call(
        functools.partial(_ret_kernel, reverse=reverse, finalize=finalize, n_chunk=tt // RET_KCHUNK),
        grid=(b, nt),
        in_specs=in_specs,
        out_specs=[tok, st],
        out_shape=[jax.ShapeDtypeStruct((b, t, w), F32),
                   jax.ShapeDtypeStruct((b, RET_HEADS, RET_DK, RET_DV), F32)],
        scratch_shapes=[pltpu.VMEM((RET_HEADS, RET_DK, RET_DV), F32)],
        compiler_params=_params(("arbitrary", "arbitrary"), VMEM_LIMIT),
        name="retention_bwd" if reverse else "retention_fwd",
    )(*ins)


def _retention(q, k, v, g, ld, s_f, s_b):
    out_f, fin_f = _ret_sweep(q, k, v, ld[0], s_f, False)
    out, fin_b = _ret_sweep(q, k, v, ld[1], s_b, True, out_f, g)
    return out, fin_f, fin_b


def _attn_kernel(*refs, n_src, tk):
    q_ref = refs[0]
    o_ref, s_buf, p_buf = refs[1 + 2 * n_src:]
    tq = q_ref.shape[0]
    chunks = []
    for si in range(n_src):
        k_ref, v_ref = refs[1 + 2 * si], refs[2 + 2 * si]
        rows = min(tk, k_ref.shape[0])
        chunks += [(k_ref, v_ref, r0, rows) for r0 in range(0, k_ref.shape[0], rows)]
    n = len(chunks)
    for hh in range(2):
        hs = slice(hh * HEAD_PAD, (hh + 1) * HEAD_PAD)
        q = q_ref[:, hs]

        def scores(c, slot):
            k_ref, _, r0, rows = chunks[c]
            s_buf[slot, :, :rows] = lax.dot_general(q, k_ref[r0:r0 + rows, hs], (((1,), (1,)), ((), ())),
                                                    preferred_element_type=F32)

        def pv(c, slot):
            _, v_ref, r0, rows = chunks[c]
            return _dot(p_buf[slot, :, :rows], v_ref[r0:r0 + rows, hs])

        scores(0, 0)
        m = jnp.full((tq, 1), -jnp.inf, F32)
        acc = jnp.zeros((tq, HEAD_PAD), F32)
        a_prev = None
        for c in range(n):
            slot = c % 2
            if c + 1 < n:
                scores(c + 1, 1 - slot)
            if c >= 1:
                acc = a_prev * acc + pv(c - 1, 1 - slot)
            rows = chunks[c][3]
            s = s_buf[slot, :, :rows]
            m_new = jnp.maximum(m, jnp.max(s, axis=-1, keepdims=True))
            a_prev = jnp.exp2(m - m_new)
            p_buf[slot, :, :rows] = jnp.exp2(s - m_new).astype(BF16)
            m = m_new
        acc = a_prev * acc + pv(n - 1, (n - 1) % 2)
        o_ref[:, hh * MLA_V:(hh + 1) * MLA_V] = (acc[:, :MLA_V] / acc[:, MLA_V:MLA_V + 1]).astype(o_ref.dtype)


def _attention(q, kvs, tq, tk):
    b, t, _ = q.shape
    in_specs = [pl.BlockSpec((None, tq, 2 * HEAD_PAD), lambda i, h, j: (i, j, h))]
    args = [q]
    for k, v in kvs:
        in_specs += [pl.BlockSpec((None, k.shape[1], 2 * HEAD_PAD), lambda i, h, j: (i, 0, h))] * 2
        args += [k, v]
    tk = min(tk, max(k.shape[1] for k, _ in kvs))
    return pl.pallas_call(
        functools.partial(_attn_kernel, n_src=len(kvs), tk=tk),
        grid=(b, MLA_HEADS // 2, t // tq),
        in_specs=in_specs,
        out_specs=pl.BlockSpec((None, tq, 2 * MLA_V), lambda i, h, j: (i, j, h)),
        out_shape=jax.ShapeDtypeStruct((b, t, MLA_VW), BF16),
        scratch_shapes=[pltpu.VMEM((2, tq, tk), F32), pltpu.VMEM((2, tq, tk), BF16)],
        compiler_params=_params(("arbitrary", "arbitrary", "arbitrary"), VMEM_LIMIT),
        name="mla_attention",
    )(*args)


def _outmlp_kernel(*refs, final, ffc):
    if final:
        (x_ref, hy_ref, ret_ref, att_ref, ga1_ref, sh2_ref, sc2_ref, ga2_ref, g2_ref, wo_ref, w1_ref, w2_ref,
         fg_ref, o_ref) = refs
    else:
        (x_ref, hy_ref, ret_ref, att_ref, ga1_ref, sh2_ref, sc2_ref, ga2_ref, g2_ref, wo_ref, w1_ref, w2_ref,
         o_ref) = refs
    w = HY_WIDTH
    mix = (_dot(hy_ref[...].astype(BF16), wo_ref[0:w, :]) + _dot(ret_ref[...].astype(BF16), wo_ref[w:2 * w, :])
           + _dot(att_ref[...], wo_ref[2 * w:, :]))
    x = x_ref[...] + ga1_ref[...] * mix
    h = (_rms(x, g2_ref[...]) * (1 + sc2_ref[...]) + sh2_ref[...]).astype(BF16)
    acc = jnp.zeros(x.shape, F32)
    for c0 in range(0, D_FF, ffc):
        hid = jnp.square(jax.nn.relu(_dot(h, w1_ref[:, c0:c0 + ffc]))).astype(BF16)
        acc = acc + _dot(hid, w2_ref[c0:c0 + ffc, :])
    x = x + ga2_ref[...] * acc
    if final:
        x = _rms(x, fg_ref[...])
    o_ref[...] = x


def _outmlp(x, hy, ret, att, ga1, sh2, sc2, ga2, lw, final_g, tm):
    b, t, d = x.shape
    tok = lambda n: pl.BlockSpec((None, tm, n), lambda i, j: (i, j, 0))
    mod = pl.BlockSpec((None, 1, d), lambda i, j: (i, 0, 0))
    final = final_g is not None
    ins = [x, hy, ret, att, ga1, sh2, sc2, ga2, lw["norm2_g"], lw["w_out"], lw["w1"], lw["w2"]]
    in_specs = [tok(d), tok(hy.shape[-1]), tok(ret.shape[-1]), tok(att.shape[-1]), mod, mod, mod, mod,
                _full((1, d)), _full(lw["w_out"].shape), _full(lw["w1"].shape), _full(lw["w2"].shape)]
    if final:
        ins.append(final_g)
        in_specs.append(_full((1, d)))
    return pl.pallas_call(
        functools.partial(_outmlp_kernel, final=final, ffc=1024),
        grid=(b, t // tm),
        in_specs=in_specs,
        out_specs=tok(d),
        out_shape=jax.ShapeDtypeStruct((b, t, d), F32),
        compiler_params=_params(("arbitrary", "arbitrary"), VMEM_LIMIT),
        name="outproj_mlp",
    )(*ins)


def _rope_tables(t):
    rows = t // GRID_W
    row = np.repeat(np.arange(rows), GRID_W).astype(np.float64)
    col = np.tile(np.arange(GRID_W), rows).astype(np.float64)
    n_freq = MLA_ROPE // 4
    inv = ROPE_BASE ** (-np.arange(n_freq) / n_freq)
    ang_r = row[:, None] * inv
    ang_c = col[:, None] * inv
    cr, sr, cc, sc = np.cos(ang_r), np.sin(ang_r), np.cos(ang_c), np.sin(ang_c)
    one = np.ones((t, MLA_NOPE))
    z64 = np.zeros((t, MLA_NOPE))
    z8 = np.zeros((t, n_freq))
    tail1 = np.ones((t, HEAD_PAD - MLA_QK))
    tail0 = np.zeros((t, HEAD_PAD - MLA_QK))
    c = np.concatenate([one, cr, cr, cc, cc, tail1], axis=1)
    s1 = np.concatenate([z64, -sr, z8, -sc, z8, tail0], axis=1)
    s2 = np.concatenate([z64, z8, sr, z8, sc, tail0], axis=1)
    return tuple(jnp.asarray(a, F32) for a in (c, s1, s2))


def _identity_rope_tables(t):
    return jnp.ones((t, HEAD_PAD), F32), jnp.zeros((t, HEAD_PAD), F32), jnp.zeros((t, HEAD_PAD), F32)


def _layer_weights(p):
    d = D_MODEL
    w_in = p["w_in"]
    hy_cols = 3 * HY_WIDTH
    ret_cols = 4 * RET_W
    w_mla = w_in[:, hy_cols + ret_cols:]
    pad_l = jnp.zeros((d, MLA_NOPE), F32)
    pad_r = jnp.zeros((d, HEAD_PAD - MLA_QK), F32)
    w_mla = jnp.concatenate([w_mla[:, :MLA_Q_LORA + MLA_KV_LORA], pad_l, w_mla[:, MLA_Q_LORA + MLA_KV_LORA:], pad_r],
                            axis=1)
    w_in_pad = jnp.concatenate([w_in[:, :hy_cols + ret_cols], w_mla], axis=1).astype(BF16)
    w_uq = p["mla_w_uq"].reshape(MLA_Q_LORA, MLA_HEADS, MLA_QK)
    w_uq = jnp.pad(w_uq, ((0, 0), (0, 0), (0, HEAD_PAD - MLA_QK))).reshape(MLA_Q_LORA, MLA_QW).astype(BF16)
    w_ukv = p["mla_w_ukv"].reshape(MLA_KV_LORA, MLA_HEADS, MLA_NOPE + MLA_V)
    w_uk = jnp.pad(w_ukv[:, :, :MLA_NOPE], ((0, 0), (0, 0), (0, HEAD_PAD - MLA_NOPE)))
    w_uk = w_uk.reshape(MLA_KV_LORA, MLA_QW).astype(BF16)
    w_uv = jnp.pad(w_ukv[:, :, MLA_NOPE:], ((0, 0), (0, 0), (0, HEAD_PAD - MLA_V)))
    w_uv = w_uv.reshape(MLA_KV_LORA, MLA_QW).astype(BF16)
    f = HY_FFN
    half = HY_ORDER * HY_WIDTH
    two = lambda a: jnp.concatenate([a, a], axis=-1)
    w1 = jnp.zeros((2 * LANE, LANE), F32)
    w1 = w1.at[:HY_EMB, :f].set(p["hy_ffn_w1"]).at[LANE:LANE + HY_EMB, f:].set(p["hy_ffn_w1"])
    w2 = jnp.zeros((LANE, LANE), F32).at[:f, :f].set(p["hy_ffn_w2"]).at[f:, f:].set(p["hy_ffn_w2"])
    fw = {
        "w1": w1, "b1": two(p["hy_ffn_b1"])[None, :], "sf": two(p["hy_sin_freq"]),
        "w2": w2, "b2": two(p["hy_ffn_b2"])[None, :],
        "w3a": jnp.pad(p["hy_ffn_w3"], ((0, f), (0, 0))),
        "w3b": jnp.pad(p["hy_ffn_w3"][:, half:], ((f, 0), (0, 0))),
    }
    ld = jnp.broadcast_to(p["ret_log_decay"][:, :, None, None], (2, RET_HEADS, 1, LANE))
    return {
        "w_in": w_in_pad, "w_uq": w_uq, "w_uk": w_uk, "w_uv": w_uv,
        "q_g": p["mla_q_norm_g"][None, :], "kv_g": p["mla_kv_norm_g"][None, :],
        "norm1_g": p["norm1_g"][None, :], "norm2_g": p["norm2_g"][None, :],
        "w_out": p["w_out"].astype(BF16), "w1": p["mlp_w1"].astype(BF16), "w2": p["mlp_w2"].astype(BF16),
        "conv_w": p["hy_conv_w"], "conv_b": p["hy_conv_b"][None, :], "hy_bias": p["hy_bias"],
        "filt": fw, "ld": ld,
    }


def _mixers(x, sh1, sc1, lw, rope_tabs, tm):
    x1, x2, v, rq, rk, rv, rg, mq, mk, mv = _inproj(x, sh1, sc1, lw["norm1_g"], lw, rope_tabs, tm)
    return (x1, x2, v), (rq, rk, rv, rg), (mq, mk, mv)


def kernel(x, c, ctx, c_ctx, w_ada, b_ada, norm1_g, norm2_g, w_in, w_out, hy_conv_w, hy_conv_b, hy_ffn_w1,
           hy_ffn_b1, hy_sin_freq, hy_ffn_w2, hy_ffn_b2, hy_ffn_w3, hy_bias, ret_log_decay, mla_q_norm_g, mla_w_uq,
           mla_kv_norm_g, mla_w_ukv, mlp_w1, mlp_w2, final_norm_g):
    b, seq, d = x.shape
    n_ctx = ctx.shape[1]
    depth = w_ada.shape[0]
    cond8 = jnp.concatenate([c, c_ctx[None, :], jnp.zeros((8 - b - 1, d), F32)], axis=0)
    ada = _ada(cond8, w_ada, b_ada)
    rope_lat = _rope_tables(seq)
    rope_ctx = _identity_rope_tables(n_ctx)
    dft_tables = _dft_tables(seq)
    zero_state = jnp.zeros((b, RET_HEADS, RET_DK, RET_DV), F32)
    xc = ctx
    for i in range(depth):
        p = {
            "w_in": w_in[i], "w_out": w_out[i], "norm1_g": norm1_g[i], "norm2_g": norm2_g[i],
            "hy_conv_w": hy_conv_w[i], "hy_conv_b": hy_conv_b[i], "hy_ffn_w1": hy_ffn_w1[i],
            "hy_ffn_b1": hy_ffn_b1[i], "hy_sin_freq": hy_sin_freq[i], "hy_ffn_w2": hy_ffn_w2[i],
            "hy_ffn_b2": hy_ffn_b2[i], "hy_ffn_w3": hy_ffn_w3[i], "hy_bias": hy_bias[i],
            "ret_log_decay": ret_log_decay[i], "mla_q_norm_g": mla_q_norm_g[i], "mla_w_uq": mla_w_uq[i],
            "mla_kv_norm_g": mla_kv_norm_g[i], "mla_w_ukv": mla_w_ukv[i], "mlp_w1": mlp_w1[i], "mlp_w2": mlp_w2[i],
        }
        lw = _layer_weights(p)
        last = i == depth - 1
        terms = [ada[i, :, k * d:(k + 1) * d] for k in range(N_MOD)]
        lat = [tm[:b, None, :] for tm in terms]
        cx = [jnp.broadcast_to(tm[b:b + 1, None, :], (b, 1, d)) for tm in terms]

        hy_c, ret_c, (cq, ck, cv) = _mixers(xc, cx[0], cx[1], lw, rope_ctx, n_ctx)
        hy_l, ret_l, (mq, mk, mv) = _mixers(x, lat[0], lat[1], lw, rope_lat, 1024)

        rq, rk, rv, rg = ret_c
        ret_ctx, s_f, s_b = _retention(rq, rk, rv, rg, lw["ld"], zero_state, zero_state)
        rq, rk, rv, rg = ret_l
        ret, _, _ = _retention(rq, rk, rv, rg, lw["ld"], s_f, s_b)

        hy = _hyena_long(*hy_l, lw["filt"], lw["hy_bias"], dft_tables)

        att = _attention(mq, [(mk, mv), (ck, cv)], 1024, ATTN_TK)

        fg = final_norm_g[None, :] if last else None
        x = _outmlp(x, hy, ret, att, lat[2], lat[3], lat[4], lat[5], lw, fg, 512)
        if not last:
            hyc = _hyena_ctx(*hy_c, lw["filt"], lw["hy_bias"])
            att_c = _attention(cq, [(ck, cv)], n_ctx, ATTN_TK)
            xc = _outmlp(xc, hyc, ret_ctx, att_c, cx[2], cx[3], cx[4], cx[5], lw, None, n_ctx)
    return x
```

```python
import functools
import math

import numpy as np
import jax
import jax.numpy as jnp
from jax import lax
from jax.experimental import pallas as pl
from jax.experimental.pallas import tpu as pltpu

F32 = jnp.float32
BF16 = jnp.bfloat16

D_MODEL = 1024
DEPTH = 2
GRID_W = 64
D_FF = 4 * D_MODEL
N_MOD = 6
EPS = 1e-6

HY_WIDTH = D_MODEL // 4
HY_ORDER = 2
HY_EMB = 33
HY_BANDS = (HY_EMB - 1) // 2
HY_FFN = 64
HY_TARGET = 1e-2
HY_FAST_PCT = 0.3
HY_SLOW_PCT = 1.5

RET_DK = 64
RET_DV = 64
RET_HEADS = (D_MODEL // 4) // RET_DV
RET_KCHUNK = 256
RET_W = RET_HEADS * RET_DK

MLA_V = 64
MLA_HEADS = (D_MODEL // 2) // MLA_V
MLA_NOPE = 64
MLA_ROPE = 32
MLA_QK = MLA_NOPE + MLA_ROPE
MLA_Q_LORA = D_MODEL // 4
MLA_KV_LORA = D_MODEL // 8
ROPE_BASE = 10000.0

LANE = 128
HEAD_PAD = LANE
MLA_QW = MLA_HEADS * HEAD_PAD
MLA_VW = MLA_HEADS * MLA_V
MLA_IN_PAD = MLA_Q_LORA + MLA_KV_LORA + HEAD_PAD
IN_PAD = 3 * HY_WIDTH + 4 * RET_W + MLA_IN_PAD
Q_SCALE = MLA_QK ** -0.5 * math.log2(math.e)
ATTN_TK = 1024
DFT = 128
VMEM_LIMIT = 56 * 1024 * 1024


def _params(sem, vmem=None):
    return pltpu.CompilerParams(dimension_semantics=sem, vmem_limit_bytes=vmem)


def _full(shape):
    n = len(shape)
    return pl.BlockSpec(shape, lambda *_: (0,) * n, pipeline_mode=pl.Buffered(1))


def _dot(a, b):
    return jnp.dot(a, b, preferred_element_type=F32)


def _rms(x, g):
    return x * lax.rsqrt(jnp.mean(x * x, axis=-1, keepdims=True) + EPS) * g


def _ada_kernel(c_ref, w_ref, b_ref, o_ref):
    s = jax.nn.silu(c_ref[...])
    o_ref[...] = _dot(s.astype(BF16), w_ref[...].astype(BF16)) + b_ref[...]


def _ada(cond8, w_ada, b_ada):
    depth, d, n = w_ada.shape
    tn = 1536
    return pl.pallas_call(
        _ada_kernel,
        grid=(depth, n // tn),
        in_specs=[pl.BlockSpec((8, d), lambda l, j: (0, 0)),
                  pl.BlockSpec((None, d, tn), lambda l, j: (l, 0, j)),
                  pl.BlockSpec((None, 1, tn), lambda l, j: (l, 0, j))],
        out_specs=pl.BlockSpec((None, 8, tn), lambda l, j: (l, 0, j)),
        out_shape=jax.ShapeDtypeStruct((depth, 8, n), F32),
        compiler_params=_params(("arbitrary", "arbitrary"), VMEM_LIMIT),
        name="ada",
    )(cond8, w_ada, b_ada.reshape(depth, 1, n))


def _rope(t, c, s1, s2):
    return t * c + pltpu.roll(t, LANE - 8, 1) * s1 + pltpu.roll(t, 8, 1) * s2


def _inproj_kernel(x_ref, xp_ref, xn_ref, sh_ref, sc_ref, g_ref, win_ref, cw_ref, cb_ref, qg_ref, kvg_ref, wuq_ref,
                   wuk_ref, wuv_ref, vone_ref, rc_ref, rs1_ref, rs2_ref,
                   x1_ref, x2_ref, v_ref, rq_ref, rk_ref, rv_ref, rg_ref, mq_ref, mk_ref, mv_ref):
    w = HY_WIDTH
    j = pl.program_id(1)
    tm = x_ref.shape[0]
    mod = lambda t: (_rms(t, g_ref[...]) * (1 + sc_ref[...]) + sh_ref[...]).astype(BF16)
    h = mod(x_ref[...])
    edge = _dot(mod(jnp.concatenate([xp_ref[...], xn_ref[...]], axis=0)), win_ref[:, 0:3 * w])
    before = jnp.where(j > 0, edge[7:8], 0.0)
    after = jnp.where(j < pl.num_programs(1) - 1, edge[8:9], 0.0)
    first = lax.broadcasted_iota(jnp.int32, (tm, w), 0) == 0
    last = lax.broadcasted_iota(jnp.int32, (tm, w), 0) == tm - 1
    for gi, o_ref in enumerate((x1_ref, x2_ref, v_ref)):
        cs = slice(gi * w, (gi + 1) * w)
        u = _dot(h, win_ref[:, cs])
        up = jnp.where(first, before[:, cs], pltpu.roll(u, 1, 0))
        dn = jnp.where(last, after[:, cs], pltpu.roll(u, tm - 1, 0))
        o_ref[...] = cb_ref[:, cs] + up * cw_ref[0:1, cs] + u * cw_ref[1:2, cs] + dn * cw_ref[2:3, cs]
    o = 3 * w
    rq_ref[...] = _dot(h, win_ref[:, o:o + RET_W]).astype(BF16)
    rk_ref[...] = (_dot(h, win_ref[:, o + RET_W:o + 2 * RET_W]) * RET_DK ** -0.5).astype(BF16)
    rv_ref[...] = _dot(h, win_ref[:, o + 2 * RET_W:o + 3 * RET_W]).astype(BF16)
    rg_ref[...] = _dot(h, win_ref[:, o + 3 * RET_W:o + 4 * RET_W])
    o = o + 4 * RET_W
    um = _dot(h, win_ref[:, o:o + MLA_IN_PAD])
    cq = um[:, :MLA_Q_LORA]
    ckv = um[:, MLA_Q_LORA:MLA_Q_LORA + MLA_KV_LORA]
    kr = um[:, MLA_Q_LORA + MLA_KV_LORA:]
    rc, rs1, rs2 = rc_ref[...], rs1_ref[...], rs2_ref[...]
    q = _dot(_rms(cq, qg_ref[...]).astype(BF16), wuq_ref[...])
    for hd in range(MLA_HEADS):
        sl = slice(hd * HEAD_PAD, (hd + 1) * HEAD_PAD)
        mq_ref[:, sl] = (_rope(q[:, sl], rc, rs1, rs2) * Q_SCALE).astype(BF16)
    ckvn = _rms(ckv, kvg_ref[...]).astype(BF16)
    kn = _dot(ckvn, wuk_ref[...])
    krr = _rope(kr, rc, rs1, rs2)
    for hd in range(MLA_HEADS):
        sl = slice(hd * HEAD_PAD, (hd + 1) * HEAD_PAD)
        mk_ref[:, sl] = (kn[:, sl] + krr).astype(BF16)
    mv_ref[...] = (_dot(ckvn, wuv_ref[...]) + vone_ref[...]).astype(BF16)


def _inproj(x, sh, sc, g, lw, rope_tabs, tm):
    b, t, d = x.shape
    nblk = t // 8
    tok = lambda n: pl.BlockSpec((None, tm, n), lambda i, j: (i, j, 0))
    halo_before = pl.BlockSpec((None, 8, d), lambda i, j: (i, jnp.maximum(j * (tm // 8) - 1, 0), 0))
    halo_after = pl.BlockSpec((None, 8, d), lambda i, j: (i, jnp.minimum((j + 1) * (tm // 8), nblk - 1), 0))
    mod = pl.BlockSpec((None, 1, d), lambda i, j: (i, 0, 0))
    tab = pl.BlockSpec((tm, LANE), lambda i, j: (j, 0))
    outs = [(HY_WIDTH, F32)] * 3 + [(RET_W, BF16)] * 3 + [(RET_W, F32)] + [(MLA_QW, BF16)] * 3
    vone = jnp.zeros((MLA_HEADS, HEAD_PAD), F32).at[:, MLA_V].set(1.0).reshape(1, MLA_QW)
    return pl.pallas_call(
        _inproj_kernel,
        grid=(b, t // tm),
        in_specs=[tok(d), halo_before, halo_after, mod, mod, _full((1, d)), _full(lw["w_in"].shape),
                  _full(lw["conv_w"].shape), _full(lw["conv_b"].shape), _full((1, MLA_Q_LORA)),
                  _full((1, MLA_KV_LORA)), _full(lw["w_uq"].shape), _full(lw["w_uk"].shape),
                  _full(lw["w_uv"].shape), _full((1, MLA_QW)), tab, tab, tab],
        out_specs=[tok(n) for n, _ in outs],
        out_shape=[jax.ShapeDtypeStruct((b, t, n), dt) for n, dt in outs],
        compiler_params=_params(("arbitrary", "arbitrary"), VMEM_LIMIT),
        name="inproj",
    )(x, x, x, sh, sc, g, lw["w_in"], lw["conv_w"], lw["conv_b"], lw["q_g"], lw["kv_g"], lw["w_uq"], lw["w_uk"],
      lw["w_uv"], vone, *rope_tabs)


def _filt_kernel(z_ref, w1_ref, b1_ref, sf_ref, w2_ref, b2_ref, w3a_ref, w3b_ref, dl_ref, k_ref, nrm_ref, *, rows):
    i = pl.program_id(0)
    hp = lax.Precision.HIGHEST
    z = z_ref[...]
    h = jnp.sin(sf_ref[0:1, :] * (jnp.dot(z, w1_ref[...], precision=hp, preferred_element_type=F32) + b1_ref[...]))
    h = jnp.sin(sf_ref[1:2, :] * (jnp.dot(h, w2_ref[...], precision=hp, preferred_element_type=F32) + b2_ref[...]))
    ha = jnp.dot(h, w3a_ref[...], precision=hp, preferred_element_type=F32)
    hb = jnp.dot(h, w3b_ref[...], precision=hp, preferred_element_type=F32)
    half = HY_ORDER * HY_WIDTH
    win_a = jnp.exp(-z[:, 0:1] * dl_ref[...])
    win_b = jnp.exp(-z[:, LANE:LANE + 1] * dl_ref[...])
    first = (i * rows + lax.broadcasted_iota(jnp.int32, (rows, half), 0)) == 0
    fwd = ha[:, :half] * win_a
    top = jnp.where(first, fwd + ha[:, half:] * win_a, fwd)
    bot = jnp.where(first, 0.0, hb * win_b)
    k_ref[0] = top
    k_ref[1] = bot

    @pl.when(i == 0)
    def _():
        nrm_ref[...] = jnp.zeros_like(nrm_ref)

    nrm_ref[...] += jnp.sum(jnp.abs(top), axis=0, keepdims=True) + jnp.sum(jnp.abs(bot), axis=0, keepdims=True)


def _hyena_filter(seq, fw):
    t = np.linspace(0.0, 1.0, seq)[:, None]
    bands = np.linspace(1e-4, HY_BANDS - 1, HY_BANDS)[None, :]
    ang = (2.0 * math.pi / seq) * np.arange(seq)[:, None] * bands
    z = np.concatenate([t, np.cos(ang), -np.sin(ang)], axis=-1)
    z2 = np.zeros((seq, 2 * LANE), np.float32)
    z2[:, :HY_EMB] = z
    z2[1:, LANE:LANE + HY_EMB] = z[1:][::-1]
    z2 = jnp.asarray(z2)
    max_decay = math.log(HY_TARGET) / HY_FAST_PCT
    min_decay = math.log(HY_TARGET) / HY_SLOW_PCT
    deltas = jnp.abs(jnp.linspace(min_decay, max_decay, HY_WIDTH, dtype=F32))
    dl = jnp.tile(deltas, HY_ORDER)[None, :]
    half = HY_ORDER * HY_WIDTH
    rows = min(seq, 512)
    k, nrm = pl.pallas_call(
        functools.partial(_filt_kernel, rows=rows),
        grid=(seq // rows,),
        in_specs=[pl.BlockSpec((rows, 2 * LANE), lambda i: (i, 0)), _full((2 * LANE, LANE)), _full((1, LANE)),
                  _full((2, LANE)), _full((LANE, LANE)), _full((1, LANE)), _full((LANE, 2 * half)),
                  _full((LANE, half)), _full((1, half))],
        out_specs=[pl.BlockSpec((2, rows, half), lambda i: (0, i, 0)), pl.BlockSpec((1, half), lambda i: (0, 0))],
        out_shape=[jax.ShapeDtypeStruct((2, seq, half), F32), jax.ShapeDtypeStruct((1, half), F32)],
        compiler_params=_params(("arbitrary",), VMEM_LIMIT),
        name="hyena_filter",
    )(z2, fw["w1"], fw["b1"], fw["sf"], fw["w2"], fw["b2"], fw["w3a"], fw["w3b"], dl)
    return k.reshape(2 * seq, half), nrm


def _dft_tables(seq):
    assert 2 * seq == DFT * DFT
    n_a = seq // DFT
    idx = np.arange(DFT)
    ang = 2.0 * np.pi * ((idx[:, None] * idx[None, :]) % DFT) / DFT
    fr, fi = jnp.asarray(np.cos(ang), F32), jnp.asarray(-np.sin(ang), F32)
    tang = 2.0 * np.pi * (idx[:, None] * idx[None, :]) / (DFT * DFT)
    tr, ti = jnp.asarray(np.cos(tang), F32), jnp.asarray(-np.sin(tang), F32)
    c = tr[:, :, None] * fr[None] - ti[:, :, None] * fi[None]
    sn = tr[:, :, None] * fi[None] + ti[:, :, None] * fr[None]
    block = lambda p, q, r, s, ax: jnp.concatenate([jnp.concatenate([p, q], axis=ax + 1),
                                                    jnp.concatenate([r, s], axis=ax + 1)], axis=ax).astype(BF16)
    cd, sd = c[:, :, :n_a], sn[:, :, :n_a]
    ct, st = jnp.swapaxes(cd, 1, 2), jnp.swapaxes(sd, 1, 2)
    return {
        "fwd": block(fr, -fi, fi, fr, 0), "inv": block(fr, fi, -fi, fr, 0),
        "s1_real": jnp.concatenate([c, sn], axis=1).astype(BF16),
        "s1": block(cd, -sd, sd, cd, 1),
        "s3": block(ct, st, -st, ct, 1),
    }


def _cmul(ar, ai, br, bi):
    return ar * br - ai * bi, ar * bi + ai * br


def _fft1_kernel(x_ref, f_ref, or_ref, oi_ref, *, n_b, has_im):
    if has_im:
        x = jnp.concatenate([jnp.swapaxes(x_ref[0], 0, 1), jnp.swapaxes(x_ref[1], 0, 1)], axis=1)
    else:
        x = jnp.swapaxes(x_ref[...], 0, 1)
    outs_r, outs_i = [], []
    for j in range(n_b):
        o = _dot(f_ref[j], x[j].astype(BF16))
        outs_r.append(o[:DFT].astype(BF16))
        outs_i.append(o[DFT:].astype(BF16))
    or_ref[...] = jnp.swapaxes(jnp.stack(outs_r, axis=0), 0, 1)
    oi_ref[...] = jnp.swapaxes(jnp.stack(outs_i, axis=0), 0, 1)


def _fft1(x, f1, cw, has_im, n_b=16):
    g = x.shape[0]
    n_a = x.shape[-3]
    if has_im:
        xspec = pl.BlockSpec((None, 2, n_a, n_b, cw), lambda j, i: (i, 0, 0, j, 0))
    else:
        xspec = pl.BlockSpec((None, n_a, n_b, cw), lambda j, i: (i, 0, j, 0))
    ospec = pl.BlockSpec((None, DFT, n_b, cw), lambda j, i: (i, 0, j, 0))
    oshape = jax.ShapeDtypeStruct((g, DFT, DFT, cw), BF16)
    return pl.pallas_call(
        functools.partial(_fft1_kernel, n_b=n_b, has_im=has_im),
        grid=(DFT // n_b, g),
        in_specs=[xspec, pl.BlockSpec((n_b,) + f1.shape[1:], lambda j, i: (j, 0, 0))],
        out_specs=[ospec, ospec],
        out_shape=[oshape, oshape],
        compiler_params=_params(("arbitrary", "arbitrary"), VMEM_LIMIT),
        name="fft_stage1",
    )(x, f1)


def _fft2_filter_kernel(ar_ref, ai_ref, f_ref, nrm_ref, kr_ref, ki_ref, *, qb):
    f = f_ref[...]
    scale = 1.0 / (nrm_ref[...] * float(DFT * DFT))
    for qi in range(qb):
        o = _dot(f, jnp.concatenate([ar_ref[qi], ai_ref[qi]], axis=0))
        kr_ref[qi] = (o[:DFT] * scale).astype(BF16)
        ki_ref[qi] = (o[DFT:] * scale).astype(BF16)


def _fft2_filter(ar, ai, fwd, nrm, cw, qb=4):
    spec = pl.BlockSpec((qb, DFT, cw), lambda i: (i, 0, 0))
    oshape = jax.ShapeDtypeStruct((DFT, DFT, cw), BF16)
    return pl.pallas_call(
        functools.partial(_fft2_filter_kernel, qb=qb),
        grid=(DFT // qb,),
        in_specs=[spec, spec, _full(fwd.shape), _full((1, cw))],
        out_specs=[spec, spec],
        out_shape=[oshape, oshape],
        compiler_params=_params(("arbitrary",), VMEM_LIMIT),
        name="fft_filter_stage2",
    )(ar.reshape(DFT, DFT, cw), ai.reshape(DFT, DFT, cw), fwd, nrm)


def _fft2_kernel(ar_ref, ai_ref, fwd_ref, inv_ref, kr_ref, ki_ref, br_ref, bi_ref, *, qb, groups, cw):
    fwd, inv = fwd_ref[...], inv_ref[...]
    for qi in range(qb):
        a = jnp.concatenate([jnp.concatenate([ar_ref[g, qi], ai_ref[g, qi]], axis=0) for g in range(groups)], axis=1)
        x = _dot(fwd, a)
        kr = jnp.concatenate([kr_ref[qi]] * groups, axis=1)
        ki = jnp.concatenate([ki_ref[qi]] * groups, axis=1)
        yr, yi = _cmul(x[:DFT], x[DFT:], kr, ki)
        o = _dot(inv, jnp.concatenate([yr, yi], axis=0).astype(BF16))
        for g in range(groups):
            br_ref[g, qi] = o[:DFT, g * cw:(g + 1) * cw].astype(BF16)
            bi_ref[g, qi] = o[DFT:, g * cw:(g + 1) * cw].astype(BF16)


def _fft2(ar, ai, fwd, inv, kr, ki, order, cw, qb=8):
    g = ar.shape[0]
    a4 = pl.BlockSpec((g, qb, DFT, cw), lambda i: (0, i, 0, 0))
    ksp = pl.BlockSpec((qb, DFT, cw), lambda i: (i, 0, order))
    oshape = jax.ShapeDtypeStruct((g, DFT, DFT, cw), BF16)
    return pl.pallas_call(
        functools.partial(_fft2_kernel, qb=qb, groups=g, cw=cw),
        grid=(DFT // qb,),
        in_specs=[a4, a4, _full(fwd.shape), _full(inv.shape), ksp, ksp],
        out_specs=[a4, a4],
        out_shape=[oshape, oshape],
        compiler_params=_params(("arbitrary",), VMEM_LIMIT),
        name="fft_stage2",
    )(ar, ai, fwd, inv, kr, ki)


def _fft3_kernel(br_ref, bi_ref, f_ref, z_ref, gate_ref, bias_ref, o_ref, *, n_b, n_a):
    b = jnp.concatenate([jnp.swapaxes(br_ref[...], 0, 1), jnp.swapaxes(bi_ref[...], 0, 1)], axis=1)
    y0, y1 = [], []
    for j in range(n_b):
        o = _dot(f_ref[j], b[j])
        y0.append(o[:n_a])
        y1.append(o[n_a:])
    bias = bias_ref[...]
    for s, ys in enumerate((y0, y1)):
        y = jnp.swapaxes(jnp.stack(ys, axis=0), 0, 1)
        o_ref[s] = gate_ref[s] * (y + z_ref[s] * bias)


def _fft3(br, bi, f3, z, gate, bias, cw, n_b=16):
    g = br.shape[0]
    n_a = f3.shape[1] // 2
    bsp = pl.BlockSpec((None, DFT, n_b, cw), lambda j, i: (i, 0, j, 0))
    zsp = pl.BlockSpec((None, 2, n_a, n_b, cw), lambda j, i: (i, 0, 0, j, 0))
    return pl.pallas_call(
        functools.partial(_fft3_kernel, n_b=n_b, n_a=n_a),
        grid=(DFT // n_b, g),
        in_specs=[bsp, bsp, pl.BlockSpec((n_b,) + f3.shape[1:], lambda j, i: (j, 0, 0)), zsp, zsp, _full((1, cw))],
        out_specs=zsp,
        out_shape=jax.ShapeDtypeStruct(z.shape, F32),
        compiler_params=_params(("arbitrary", "arbitrary"), VMEM_LIMIT),
        name="fft_stage3",
    )(br, bi, f3, z, gate, bias)


def _hyena_long(x1, x2, v, fw, bias, tables):
    b, seq, w = v.shape
    assert b % 2 == 0
    n_a = seq // DFT
    kf, nrm = _hyena_filter(seq, fw)
    half = HY_ORDER * w
    far, fai = _fft1(kf.reshape(1, DFT, DFT, half), tables["s1_real"], half, False)
    kr, ki = _fft2_filter(far, fai, tables["fwd"], nrm, half)
    view = lambda a: a.reshape(b // 2, 2, n_a, DFT, w)
    z, gates = view(v), (view(x1), view(x2))
    for o in range(HY_ORDER):
        ar, ai = _fft1(z, tables["s1"], w, True)
        br, bi = _fft2(ar, ai, tables["fwd"], tables["inv"], kr, ki, o, w)
        z = _fft3(br, bi, tables["s3"], z, gates[o], bias[o:o + 1], w)
    return z.reshape(b, seq, w)


def _ctx_hyena_kernel(x1_ref, x2_ref, v_ref, kf_ref, nrm_ref, fd_ref, fi_ref, bias_ref, o_ref, *, seq, w):
    n = 2 * seq
    fd = fd_ref[...]
    kc = _dot(fd, kf_ref[...].astype(BF16))
    scale = 1.0 / (nrm_ref[...] * float(n))
    kr, ki = kc[:n] * scale, kc[n:] * scale
    finv = fi_ref[...]
    z = v_ref[...]
    gates = (x1_ref[...], x2_ref[...])
    for o in range(HY_ORDER):
        sl = slice(o * w, (o + 1) * w)
        xf = _dot(fd[:, :seq], z.astype(BF16))
        yr, yi = _cmul(xf[:n], xf[n:], kr[:, sl], ki[:, sl])
        y = _dot(finv, jnp.concatenate([yr, yi], axis=0).astype(BF16))
        z = gates[o] * (y + z * bias_ref[o:o + 1, :])
    o_ref[...] = z


def _hyena_ctx(x1, x2, v, fw, bias):
    b, seq, w = v.shape
    n = 2 * seq
    idx = np.arange(n)
    ang = 2.0 * np.pi * ((idx[:, None] * idx[None, :]) % n) / n
    fr, fi = np.cos(ang), -np.sin(ang)
    fd = jnp.asarray(np.concatenate([fr, fi], axis=0), F32).astype(BF16)
    finv = jnp.asarray(np.concatenate([fr[:seq], fi[:seq]], axis=1), F32).astype(BF16)
    kf, nrm = _hyena_filter(seq, fw)
    tok = pl.BlockSpec((None, seq, w), lambda i: (i, 0, 0))
    return pl.pallas_call(
        functools.partial(_ctx_hyena_kernel, seq=seq, w=w),
        grid=(b,),
        in_specs=[tok, tok, tok, _full(kf.shape), _full(nrm.shape), _full(fd.shape), _full(finv.shape),
                  _full(bias.shape)],
        out_specs=tok,
        out_shape=jax.ShapeDtypeStruct(v.shape, F32),
        compiler_params=_params(("arbitrary",), VMEM_LIMIT),
        name="hyena_ctx",
    )(x1, x2, v, kf, nrm, fd, finv, bias)


def _ret_kernel(*refs, reverse, finalize, n_chunk):
    if finalize:
        q_ref, k_ref, v_ref, ld_ref, s0_ref, of_ref, g_ref, avg_ref, out_ref, sfin_ref, s_scr = refs
    else:
        q_ref, k_ref, v_ref, ld_ref, s0_ref, out_ref, sfin_ref, s_scr = refs
    c = q_ref.shape[0] // n_chunk
    t = pl.program_id(1)

    @pl.when(t == 0)
    def _():
        s_scr[...] = s0_ref[...]

    r = lax.broadcasted_iota(jnp.int32, (c, c), 0)
    m = lax.broadcasted_iota(jnp.int32, (c, c), 1)
    diff = ((m - r) if reverse else (r - m)).astype(F32)
    pos = lax.broadcasted_iota(jnp.int32, (c, RET_DV), 0).astype(F32)
    order = range(n_chunk - 1, -1, -1) if reverse else range(n_chunk)
    for hd in range(RET_HEADS):
        lg = jnp.log1p(-jnp.exp(ld_ref[hd]))
        lgv = lg[:, :RET_DV]
        lgc = jnp.concatenate([lg] * (c // LANE), axis=1)
        decay = jnp.where(diff >= 0, jnp.exp(lgc * jnp.maximum(diff, 0.0)), 0.0)
        if reverse:
            zeta = jnp.exp(lgv * pos)
            xi = jnp.exp(lgv * (c - pos))
        else:
            zeta = jnp.exp(lgv * (c - 1 - pos))
            xi = jnp.exp(lgv * (pos + 1))
        g_chunk = jnp.exp(lgv * c)
        hs = slice(hd * RET_DK, (hd + 1) * RET_DK)
        state = s_scr[hd]
        entering = {}
        for ci in order:
            rows = slice(ci * c, (ci + 1) * c)
            kv = lax.dot_general(k_ref[rows, hs].astype(BF16), (v_ref[rows, hs] * zeta).astype(BF16),
                                 (((0,), (0,)), ((), ())), preferred_element_type=F32)
            entering[ci] = state
            state = g_chunk * state + kv
        s_scr[hd] = state
        for ci in order:
            rows = slice(ci * c, (ci + 1) * c)
            qh = q_ref[rows, hs].astype(BF16)
            kh = k_ref[rows, hs].astype(BF16)
            s = lax.dot_general(qh, kh, (((1,), (1,)), ((), ())), preferred_element_type=F32)
            inner = _dot((s * decay).astype(BF16), v_ref[rows, hs].astype(BF16))
            o = inner + _dot(qh, entering[ci].astype(BF16)) * xi
            out_ref[rows, hs] = o

    if finalize:
        o = out_ref[...] + of_ref[...]
        ms = _dot((o * o).astype(BF16), avg_ref[...])
        out_ref[...] = jax.nn.silu(g_ref[...]) * (o * lax.rsqrt(ms + EPS))

    @pl.when(t == pl.num_programs(1) - 1)
    def _():
        sfin_ref[...] = s_scr[...]


def _ret_sweep(q, k, v, ld, s0, reverse, fwd_out=None, gate=None):
    b, t, w = q.shape
    tt = min(t, 1024)
    nt = t // tt
    finalize = fwd_out is not None
    tmap = (lambda i, j: (i, nt - 1 - j, 0)) if reverse else (lambda i, j: (i, j, 0))
    tok = pl.BlockSpec((None, tt, w), tmap)
    st = pl.BlockSpec((None, RET_HEADS, RET_DK, RET_DV), lambda i, j: (i, 0, 0, 0))
    ins = [q, k, v, ld, s0]
    in_specs = [tok, tok, tok, _full(ld.shape), st]
    if finalize:
        avg = jnp.kron(jnp.eye(RET_HEADS, dtype=F32), jnp.full((RET_DV, RET_DV), 1.0 / RET_DV, F32)).astype(BF16)
        ins += [fwd_out, gate, avg]
        in_specs += [tok, tok, _full(avg.shape)]
    return pl.pallas_call(
        functools.partial(_ret_kernel, reverse=reverse, finalize=finalize, n_chunk=tt // RET_KCHUNK),
        grid=(b, nt),
        in_specs=in_specs,
        out_specs=[tok, st],
        out_shape=[jax.ShapeDtypeStruct((b, t, w), F32),
                   jax.ShapeDtypeStruct((b, RET_HEADS, RET_DK, RET_DV), F32)],
        scratch_shapes=[pltpu.VMEM((RET_HEADS, RET_DK, RET_DV), F32)],
        compiler_params=_params(("arbitrary", "arbitrary"), VMEM_LIMIT),
        name="retention_bwd" if reverse else "retention_fwd",
    )(*ins)


def _retention(q, k, v, g, ld, s_f, s_b):
    out_f, fin_f = _ret_sweep(q, k, v, ld[0], s_f, False)
    out, fin_b = _ret_sweep(q, k, v, ld[1], s_b, True, out_f, g)
    return out, fin_f, fin_b


def _attn_kernel(*refs, n_src, tk):
    q_ref = refs[0]
    o_ref, s_buf, p_buf = refs[1 + 2 * n_src:]
    tq = q_ref.shape[0]
    chunks = []
    for si in range(n_src):
        k_ref, v_ref = refs[1 + 2 * si], refs[2 + 2 * si]
        rows = min(tk, k_ref.shape[0])
        chunks += [(k_ref, v_ref, r0, rows) for r0 in range(0, k_ref.shape[0], rows)]
    n = len(chunks)
    for hh in range(2):
        hs = slice(hh * HEAD_PAD, (hh + 1) * HEAD_PAD)
        q = q_ref[:, hs]

        def scores(c, slot):
            k_ref, _, r0, rows = chunks[c]
            s_buf[slot, :, :rows] = lax.dot_general(q, k_ref[r0:r0 + rows, hs], (((1,), (1,)), ((), ())),
                                                    preferred_element_type=F32)

        def pv(c, slot):
            _, v_ref, r0, rows = chunks[c]
            return _dot(p_buf[slot, :, :rows], v_ref[r0:r0 + rows, hs])

        scores(0, 0)
        m = jnp.full((tq, 1), -jnp.inf, F32)
        acc = jnp.zeros((tq, HEAD_PAD), F32)
        a_prev = None
        for c in range(n):
            slot = c % 2
            if c + 1 < n:
                scores(c + 1, 1 - slot)
            if c >= 1:
                acc = a_prev * acc + pv(c - 1, 1 - slot)
            rows = chunks[c][3]
            s = s_buf[slot, :, :rows]
            m_new = jnp.maximum(m, jnp.max(s, axis=-1, keepdims=True))
            a_prev = jnp.exp2(m - m_new)
            p_buf[slot, :, :rows] = jnp.exp2(s - m_new).astype(BF16)
            m = m_new
        acc = a_prev * acc + pv(n - 1, (n - 1) % 2)
        o_ref[:, hh * MLA_V:(hh + 1) * MLA_V] = (acc[:, :MLA_V] / acc[:, MLA_V:MLA_V + 1]).astype(o_ref.dtype)


def _attention(q, kvs, tq, tk):
    b, t, _ = q.shape
    in_specs = [pl.BlockSpec((None, tq, 2 * HEAD_PAD), lambda i, h, j: (i, j, h))]
    args = [q]
    for k, v in kvs:
        in_specs += [pl.BlockSpec((None, k.shape[1], 2 * HEAD_PAD), lambda i, h, j: (i, 0, h))] * 2
        args += [k, v]
    tk = min(tk, max(k.shape[1] for k, _ in kvs))
    return pl.pallas_call(
        functools.partial(_attn_kernel, n_src=len(kvs), tk=tk),
        grid=(b, MLA_HEADS // 2, t // tq),
        in_specs=in_specs,
        out_specs=pl.BlockSpec((None, tq, 2 * MLA_V), lambda i, h, j: (i, j, h)),
        out_shape=jax.ShapeDtypeStruct((b, t, MLA_VW), BF16),
        scratch_shapes=[pltpu.VMEM((2, tq, tk), F32), pltpu.VMEM((2, tq, tk), BF16)],
        compiler_params=_params(("arbitrary", "arbitrary", "arbitrary"), VMEM_LIMIT),
        name="mla_attention",
    )(*args)


def _outmlp_kernel(*refs, final, ffc):
    if final:
        (x_ref, hy_ref, ret_ref, att_ref, ga1_ref, sh2_ref, sc2_ref, ga2_ref, g2_ref, wo_ref, w1_ref, w2_ref,
         fg_ref, o_ref) = refs
    else:
        (x_ref, hy_ref, ret_ref, att_ref, ga1_ref, sh2_ref, sc2_ref, ga2_ref, g2_ref, wo_ref, w1_ref, w2_ref,
         o_ref) = refs
    w = HY_WIDTH
    mix = (_dot(hy_ref[...].astype(BF16), wo_ref[0:w, :]) + _dot(ret_ref[...].astype(BF16), wo_ref[w:2 * w, :])
           + _dot(att_ref[...], wo_ref[2 * w:, :]))
    x = x_ref[...] + ga1_ref[...] * mix
    h = (_rms(x, g2_ref[...]) * (1 + sc2_ref[...]) + sh2_ref[...]).astype(BF16)
    acc = jnp.zeros(x.shape, F32)
    for c0 in range(0, D_FF, ffc):
        hid = jnp.square(jax.nn.relu(_dot(h, w1_ref[:, c0:c0 + ffc]))).astype(BF16)
        acc = acc + _dot(hid, w2_ref[c0:c0 + ffc, :])
    x = x + ga2_ref[...] * acc
    if final:
        x = _rms(x, fg_ref[...])
    o_ref[...] = x


def _outmlp(x, hy, ret, att, ga1, sh2, sc2, ga2, lw, final_g, tm):
    b, t, d = x.shape
    tok = lambda n: pl.BlockSpec((None, tm, n), lambda i, j: (i, j, 0))
    mod = pl.BlockSpec((None, 1, d), lambda i, j: (i, 0, 0))
    final = final_g is not None
    ins = [x, hy, ret, att, ga1, sh2, sc2, ga2, lw["norm2_g"], lw["w_out"], lw["w1"], lw["w2"]]
    in_specs = [tok(d), tok(hy.shape[-1]), tok(ret.shape[-1]), tok(att.shape[-1]), mod, mod, mod, mod,
                _full((1, d)), _full(lw["w_out"].shape), _full(lw["w1"].shape), _full(lw["w2"].shape)]
    if final:
        ins.append(final_g)
        in_specs.append(_full((1, d)))
    return pl.pallas_call(
        functools.partial(_outmlp_kernel, final=final, ffc=1024),
        grid=(b, t // tm),
        in_specs=in_specs,
        out_specs=tok(d),
        out_shape=jax.ShapeDtypeStruct((b, t, d), F32),
        compiler_params=_params(("arbitrary", "arbitrary"), VMEM_LIMIT),
        name="outproj_mlp",
    )(*ins)


def _rope_tables(t):
    rows = t // GRID_W
    row = np.repeat(np.arange(rows), GRID_W).astype(np.float64)
    col = np.tile(np.arange(GRID_W), rows).astype(np.float64)
    n_freq = MLA_ROPE // 4
    inv = ROPE_BASE ** (-np.arange(n_freq) / n_freq)
    ang_r = row[:, None] * inv
    ang_c = col[:, None] * inv
    cr, sr, cc, sc = np.cos(ang_r), np.sin(ang_r), np.cos(ang_c), np.sin(ang_c)
    one = np.ones((t, MLA_NOPE))
    z64 = np.zeros((t, MLA_NOPE))
    z8 = np.zeros((t, n_freq))
    tail1 = np.ones((t, HEAD_PAD - MLA_QK))
    tail0 = np.zeros((t, HEAD_PAD - MLA_QK))
    c = np.concatenate([one, cr, cr, cc, cc, tail1], axis=1)
    s1 = np.concatenate([z64, -sr, z8, -sc, z8, tail0], axis=1)
    s2 = np.concatenate([z64, z8, sr, z8, sc, tail0], axis=1)
    return tuple(jnp.asarray(a, F32) for a in (c, s1, s2))


def _identity_rope_tables(t):
    return jnp.ones((t, HEAD_PAD), F32), jnp.zeros((t, HEAD_PAD), F32), jnp.zeros((t, HEAD_PAD), F32)


def _layer_weights(p):
    d = D_MODEL
    w_in = p["w_in"]
    hy_cols = 3 * HY_WIDTH
    ret_cols = 4 * RET_W
    main = hy_cols + ret_cols + MLA_Q_LORA + MLA_KV_LORA
    w_in_pad = jnp.zeros((d, IN_PAD), BF16).at[:, :main].set(w_in[:, :main].astype(BF16))
    w_in_pad = w_in_pad.at[:, main + MLA_NOPE:main + MLA_QK].set(w_in[:, main:].astype(BF16))
    w_uq = p["mla_w_uq"].reshape(MLA_Q_LORA, MLA_HEADS, MLA_QK)
    w_uq = jnp.pad(w_uq, ((0, 0), (0, 0), (0, HEAD_PAD - MLA_QK))).reshape(MLA_Q_LORA, MLA_QW).astype(BF16)
    w_ukv = p["mla_w_ukv"].reshape(MLA_KV_LORA, MLA_HEADS, MLA_NOPE + MLA_V)
    w_uk = jnp.pad(w_ukv[:, :, :MLA_NOPE], ((0, 0), (0, 0), (0, HEAD_PAD - MLA_NOPE)))
    w_uk = w_uk.reshape(MLA_KV_LORA, MLA_QW).astype(BF16)
    w_uv = jnp.pad(w_ukv[:, :, MLA_NOPE:], ((0, 0), (0, 0), (0, HEAD_PAD - MLA_V)))
    w_uv = w_uv.reshape(MLA_KV_LORA, MLA_QW).astype(BF16)
    f = HY_FFN
    half = HY_ORDER * HY_WIDTH
    two = lambda a: jnp.concatenate([a, a], axis=-1)
    w1 = jnp.zeros((2 * LANE, LANE), F32)
    w1 = w1.at[:HY_EMB, :f].set(p["hy_ffn_w1"]).at[LANE:LANE + HY_EMB, f:].set(p["hy_ffn_w1"])
    w2 = jnp.zeros((LANE, LANE), F32).at[:f, :f].set(p["hy_ffn_w2"]).at[f:, f:].set(p["hy_ffn_w2"])
    fw = {
        "w1": w1, "b1": two(p["hy_ffn_b1"])[None, :], "sf": two(p["hy_sin_freq"]),
        "w2": w2, "b2": two(p["hy_ffn_b2"])[None, :],
        "w3a": jnp.pad(p["hy_ffn_w3"], ((0, f), (0, 0))),
        "w3b": jnp.pad(p["hy_ffn_w3"][:, half:], ((f, 0), (0, 0))),
    }
    ld = jnp.broadcast_to(p["ret_log_decay"][:, :, None, None], (2, RET_HEADS, 1, LANE))
    return {
        "w_in": w_in_pad, "w_uq": w_uq, "w_uk": w_uk, "w_uv": w_uv,
        "q_g": p["mla_q_norm_g"][None, :], "kv_g": p["mla_kv_norm_g"][None, :],
        "norm1_g": p["norm1_g"][None, :], "norm2_g": p["norm2_g"][None, :],
        "w_out": p["w_out"].astype(BF16), "w1": p["mlp_w1"].astype(BF16), "w2": p["mlp_w2"].astype(BF16),
        "conv_w": p["hy_conv_w"], "conv_b": p["hy_conv_b"][None, :], "hy_bias": p["hy_bias"],
        "filt": fw, "ld": ld,
    }


def _mixers(x, sh1, sc1, lw, rope_tabs, tm):
    x1, x2, v, rq, rk, rv, rg, mq, mk, mv = _inproj(x, sh1, sc1, lw["norm1_g"], lw, rope_tabs, tm)
    return (x1, x2, v), (rq, rk, rv, rg), (mq, mk, mv)


def kernel(x, c, ctx, c_ctx, w_ada, b_ada, norm1_g, norm2_g, w_in, w_out, hy_conv_w, hy_conv_b, hy_ffn_w1,
           hy_ffn_b1, hy_sin_freq, hy_ffn_w2, hy_ffn_b2, hy_ffn_w3, hy_bias, ret_log_decay, mla_q_norm_g, mla_w_uq,
           mla_kv_norm_g, mla_w_ukv, mlp_w1, mlp_w2, final_norm_g):
    b, seq, d = x.shape
    n_ctx = ctx.shape[1]
    depth = w_ada.shape[0]
    cond8 = jnp.concatenate([c, c_ctx[None, :], jnp.zeros((8 - b - 1, d), F32)], axis=0)
    ada = _ada(cond8, w_ada, b_ada)
    rope_lat = _rope_tables(seq)
    rope_ctx = _identity_rope_tables(n_ctx)
    dft_tables = _dft_tables(seq)
    zero_state = jnp.zeros((b, RET_HEADS, RET_DK, RET_DV), F32)
    xc = ctx
    for i in range(depth):
        p = {
            "w_in": w_in[i], "w_out": w_out[i], "norm1_g": norm1_g[i], "norm2_g": norm2_g[i],
            "hy_conv_w": hy_conv_w[i], "hy_conv_b": hy_conv_b[i], "hy_ffn_w1": hy_ffn_w1[i],
            "hy_ffn_b1": hy_ffn_b1[i], "hy_sin_freq": hy_sin_freq[i], "hy_ffn_w2": hy_ffn_w2[i],
            "hy_ffn_b2": hy_ffn_b2[i], "hy_ffn_w3": hy_ffn_w3[i], "hy_bias": hy_bias[i],
            "ret_log_decay": ret_log_decay[i], "mla_q_norm_g": mla_q_norm_g[i], "mla_w_uq": mla_w_uq[i],
            "mla_kv_norm_g": mla_kv_norm_g[i], "mla_w_ukv": mla_w_ukv[i], "mlp_w1": mlp_w1[i], "mlp_w2": mlp_w2[i],
        }
        lw = _layer_weights(p)
        last = i == depth - 1
        terms = [ada[i, :, k * d:(k + 1) * d] for k in range(N_MOD)]
        lat = [tm[:b, None, :] for tm in terms]
        cx = [jnp.broadcast_to(tm[b:b + 1, None, :], (b, 1, d)) for tm in terms]

        hy_c, ret_c, (cq, ck, cv) = _mixers(xc, cx[0], cx[1], lw, rope_ctx, n_ctx)
        hy_l, ret_l, (mq, mk, mv) = _mixers(x, lat[0], lat[1], lw, rope_lat, 1024)

        rq, rk, rv, rg = ret_c
        ret_ctx, s_f, s_b = _retention(rq, rk, rv, rg, lw["ld"], zero_state, zero_state)
        rq, rk, rv, rg = ret_l
        ret, _, _ = _retention(rq, rk, rv, rg, lw["ld"], s_f, s_b)

        hy = _hyena_long(*hy_l, lw["filt"], lw["hy_bias"], dft_tables)

        att = _attention(mq, [(mk, mv), (ck, cv)], 1024, ATTN_TK)

        fg = final_norm_g[None, :] if last else None
        x = _outmlp(x, hy, ret, att, lat[2], lat[3], lat[4], lat[5], lw, fg, 512)
        if not last:
            hyc = _hyena_ctx(*hy_c, lw["filt"], lw["hy_bias"])
            att_c = _attention(cq, [(ck, cv)], n_ctx, ATTN_TK)
            xc = _outmlp(xc, hyc, ret_ctx, att_c, cx[2], cx[3], cx[4], cx[5], lw, None, n_ctx)
    return x
```

```python
import functools
import math

import numpy as np
import jax
import jax.numpy as jnp
from jax import lax
from jax.experimental import pallas as pl
from jax.experimental.pallas import tpu as pltpu

F32 = jnp.float32
BF16 = jnp.bfloat16

D_MODEL = 1024
DEPTH = 2
GRID_W = 64
D_FF = 4 * D_MODEL
N_MOD = 6
EPS = 1e-6

HY_WIDTH = D_MODEL // 4
HY_ORDER = 2
HY_EMB = 33
HY_BANDS = (HY_EMB - 1) // 2
HY_FFN = 64
HY_TARGET = 1e-2
HY_FAST_PCT = 0.3
HY_SLOW_PCT = 1.5

RET_DK = 64
RET_DV = 64
RET_HEADS = (D_MODEL // 4) // RET_DV
RET_KCHUNK = 256
RET_W = RET_HEADS * RET_DK

MLA_V = 64
MLA_HEADS = (D_MODEL // 2) // MLA_V
MLA_NOPE = 64
MLA_ROPE = 32
MLA_QK = MLA_NOPE + MLA_ROPE
MLA_Q_LORA = D_MODEL // 4
MLA_KV_LORA = D_MODEL // 8
ROPE_BASE = 10000.0

LANE = 128
HEAD_PAD = LANE
MLA_QW = MLA_HEADS * HEAD_PAD
MLA_VW = MLA_HEADS * MLA_V
MLA_IN_PAD = MLA_Q_LORA + MLA_KV_LORA + HEAD_PAD
IN_PAD = 3 * HY_WIDTH + 4 * RET_W + MLA_IN_PAD
Q_SCALE = MLA_QK ** -0.5 * math.log2(math.e)
ATTN_TK = 2048
DFT = 128
VMEM_LIMIT = 56 * 1024 * 1024


def _params(sem, vmem=None):
    return pltpu.CompilerParams(dimension_semantics=sem, vmem_limit_bytes=vmem)


def _full(shape):
    n = len(shape)
    return pl.BlockSpec(shape, lambda *_: (0,) * n, pipeline_mode=pl.Buffered(1))


def _dot(a, b):
    return jnp.dot(a, b, preferred_element_type=F32)


def _rms(x, g):
    return x * lax.rsqrt(jnp.mean(x * x, axis=-1, keepdims=True) + EPS) * g


def _ada_kernel(c_ref, w_ref, b_ref, o_ref):
    s = jax.nn.silu(c_ref[...])
    o_ref[...] = _dot(s.astype(BF16), w_ref[...].astype(BF16)) + b_ref[...]


def _ada(cond8, w_ada, b_ada):
    depth, d, n = w_ada.shape
    tn = 1536
    return pl.pallas_call(
        _ada_kernel,
        grid=(depth, n // tn),
        in_specs=[pl.BlockSpec((8, d), lambda l, j: (0, 0)),
                  pl.BlockSpec((None, d, tn), lambda l, j: (l, 0, j)),
                  pl.BlockSpec((None, 1, tn), lambda l, j: (l, 0, j))],
        out_specs=pl.BlockSpec((None, 8, tn), lambda l, j: (l, 0, j)),
        out_shape=jax.ShapeDtypeStruct((depth, 8, n), F32),
        compiler_params=_params(("arbitrary", "arbitrary"), VMEM_LIMIT),
        name="ada",
    )(cond8, w_ada, b_ada.reshape(depth, 1, n))


def _rope(t, c, s1, s2):
    return t * c + pltpu.roll(t, LANE - 8, 1) * s1 + pltpu.roll(t, 8, 1) * s2


def _inproj_kernel(x_ref, xp_ref, xn_ref, sh_ref, sc_ref, g_ref, win_ref, cw_ref, cb_ref, qg_ref, kvg_ref, wuq_ref,
                   wuk_ref, wuv_ref, vone_ref, rc_ref, rs1_ref, rs2_ref,
                   x1_ref, x2_ref, v_ref, rq_ref, rk_ref, rv_ref, rg_ref, mq_ref, mk_ref, mv_ref):
    w = HY_WIDTH
    j = pl.program_id(1)
    tm = x_ref.shape[0]
    mod = lambda t: (_rms(t, g_ref[...]) * (1 + sc_ref[...]) + sh_ref[...]).astype(BF16)
    h = mod(x_ref[...])
    edge = _dot(mod(jnp.concatenate([xp_ref[...], xn_ref[...]], axis=0)), win_ref[:, 0:3 * w])
    before = jnp.where(j > 0, edge[7:8], 0.0)
    after = jnp.where(j < pl.num_programs(1) - 1, edge[8:9], 0.0)
    first = lax.broadcasted_iota(jnp.int32, (tm, w), 0) == 0
    last = lax.broadcasted_iota(jnp.int32, (tm, w), 0) == tm - 1
    for gi, o_ref in enumerate((x1_ref, x2_ref, v_ref)):
        cs = slice(gi * w, (gi + 1) * w)
        u = _dot(h, win_ref[:, cs])
        up = jnp.where(first, before[:, cs], pltpu.roll(u, 1, 0))
        dn = jnp.where(last, after[:, cs], pltpu.roll(u, tm - 1, 0))
        o_ref[...] = cb_ref[:, cs] + up * cw_ref[0:1, cs] + u * cw_ref[1:2, cs] + dn * cw_ref[2:3, cs]
    o = 3 * w
    rq_ref[...] = _dot(h, win_ref[:, o:o + RET_W]).astype(BF16)
    rk_ref[...] = (_dot(h, win_ref[:, o + RET_W:o + 2 * RET_W]) * RET_DK ** -0.5).astype(BF16)
    rv_ref[...] = _dot(h, win_ref[:, o + 2 * RET_W:o + 3 * RET_W]).astype(BF16)
    rg_ref[...] = _dot(h, win_ref[:, o + 3 * RET_W:o + 4 * RET_W])
    o = o + 4 * RET_W
    um = _dot(h, win_ref[:, o:o + MLA_IN_PAD])
    cq = um[:, :MLA_Q_LORA]
    ckv = um[:, MLA_Q_LORA:MLA_Q_LORA + MLA_KV_LORA]
    kr = um[:, MLA_Q_LORA + MLA_KV_LORA:]
    rc, rs1, rs2 = rc_ref[...], rs1_ref[...], rs2_ref[...]
    q = _dot(_rms(cq, qg_ref[...]).astype(BF16), wuq_ref[...])
    for hd in range(MLA_HEADS):
        sl = slice(hd * HEAD_PAD, (hd + 1) * HEAD_PAD)
        mq_ref[:, sl] = (_rope(q[:, sl], rc, rs1, rs2) * Q_SCALE).astype(BF16)
    ckvn = _rms(ckv, kvg_ref[...]).astype(BF16)
    kn = _dot(ckvn, wuk_ref[...])
    krr = _rope(kr, rc, rs1, rs2)
    for hd in range(MLA_HEADS):
        sl = slice(hd * HEAD_PAD, (hd + 1) * HEAD_PAD)
        mk_ref[:, sl] = (kn[:, sl] + krr).astype(BF16)
    mv_ref[...] = (_dot(ckvn, wuv_ref[...]) + vone_ref[...]).astype(BF16)


def _inproj(x, sh, sc, g, lw, rope_tabs, tm):
    b, t, d = x.shape
    nblk = t // 8
    tok = lambda n: pl.BlockSpec((None, tm, n), lambda i, j: (i, j, 0))
    halo_before = pl.BlockSpec((None, 8, d), lambda i, j: (i, jnp.maximum(j * (tm // 8) - 1, 0), 0))
    halo_after = pl.BlockSpec((None, 8, d), lambda i, j: (i, jnp.minimum((j + 1) * (tm // 8), nblk - 1), 0))
    mod = pl.BlockSpec((None, 1, d), lambda i, j: (i, 0, 0))
    tab = pl.BlockSpec((tm, LANE), lambda i, j: (j, 0))
    outs = [(HY_WIDTH, F32)] * 3 + [(RET_W, BF16)] * 3 + [(RET_W, F32)] + [(MLA_QW, BF16)] * 3
    vone = jnp.zeros((MLA_HEADS, HEAD_PAD), F32).at[:, MLA_V].set(1.0).reshape(1, MLA_QW)
    return pl.pallas_call(
        _inproj_kernel,
        grid=(b, t // tm),
        in_specs=[tok(d), halo_before, halo_after, mod, mod, _full((1, d)), _full(lw["w_in"].shape),
                  _full(lw["conv_w"].shape), _full(lw["conv_b"].shape), _full((1, MLA_Q_LORA)),
                  _full((1, MLA_KV_LORA)), _full(lw["w_uq"].shape), _full(lw["w_uk"].shape),
                  _full(lw["w_uv"].shape), _full((1, MLA_QW)), tab, tab, tab],
        out_specs=[tok(n) for n, _ in outs],
        out_shape=[jax.ShapeDtypeStruct((b, t, n), dt) for n, dt in outs],
        compiler_params=_params(("arbitrary", "arbitrary"), VMEM_LIMIT),
        name="inproj",
    )(x, x, x, sh, sc, g, lw["w_in"], lw["conv_w"], lw["conv_b"], lw["q_g"], lw["kv_g"], lw["w_uq"], lw["w_uk"],
      lw["w_uv"], vone, *rope_tabs)


def _filt_kernel(z_ref, w1_ref, b1_ref, sf_ref, w2_ref, b2_ref, w3a_ref, w3b_ref, dl_ref, k_ref, nrm_ref, *, rows):
    i = pl.program_id(0)
    hp = lax.Precision.HIGHEST
    z = z_ref[...]
    h = jnp.sin(sf_ref[0:1, :] * (jnp.dot(z, w1_ref[...], precision=hp, preferred_element_type=F32) + b1_ref[...]))
    h = jnp.sin(sf_ref[1:2, :] * (jnp.dot(h, w2_ref[...], precision=hp, preferred_element_type=F32) + b2_ref[...]))
    ha = jnp.dot(h, w3a_ref[...], precision=hp, preferred_element_type=F32)
    hb = jnp.dot(h, w3b_ref[...], precision=hp, preferred_element_type=F32)
    half = HY_ORDER * HY_WIDTH
    win_a = jnp.exp(-z[:, 0:1] * dl_ref[...])
    win_b = jnp.exp(-z[:, LANE:LANE + 1] * dl_ref[...])
    first = (i * rows + lax.broadcasted_iota(jnp.int32, (rows, half), 0)) == 0
    fwd = ha[:, :half] * win_a
    top = jnp.where(first, fwd + ha[:, half:] * win_a, fwd)
    bot = jnp.where(first, 0.0, hb * win_b)
    k_ref[0] = top
    k_ref[1] = bot

    @pl.when(i == 0)
    def _():
        nrm_ref[...] = jnp.zeros_like(nrm_ref)

    nrm_ref[...] += jnp.sum(jnp.abs(top), axis=0, keepdims=True) + jnp.sum(jnp.abs(bot), axis=0, keepdims=True)


def _hyena_filter(seq, fw):
    t = np.linspace(0.0, 1.0, seq)[:, None]
    bands = np.linspace(1e-4, HY_BANDS - 1, HY_BANDS)[None, :]
    ang = (2.0 * math.pi / seq) * np.arange(seq)[:, None] * bands
    z = np.concatenate([t, np.cos(ang), -np.sin(ang)], axis=-1)
    z2 = np.zeros((seq, 2 * LANE), np.float32)
    z2[:, :HY_EMB] = z
    z2[1:, LANE:LANE + HY_EMB] = z[1:][::-1]
    z2 = jnp.asarray(z2)
    max_decay = math.log(HY_TARGET) / HY_FAST_PCT
    min_decay = math.log(HY_TARGET) / HY_SLOW_PCT
    deltas = jnp.abs(jnp.linspace(min_decay, max_decay, HY_WIDTH, dtype=F32))
    dl = jnp.tile(deltas, HY_ORDER)[None, :]
    half = HY_ORDER * HY_WIDTH
    rows = min(seq, 512)
    k, nrm = pl.pallas_call(
        functools.partial(_filt_kernel, rows=rows),
        grid=(seq // rows,),
        in_specs=[pl.BlockSpec((rows, 2 * LANE), lambda i: (i, 0)), _full((2 * LANE, LANE)), _full((1, LANE)),
                  _full((2, LANE)), _full((LANE, LANE)), _full((1, LANE)), _full((LANE, 2 * half)),
                  _full((LANE, half)), _full((1, half))],
        out_specs=[pl.BlockSpec((2, rows, half), lambda i: (0, i, 0)), pl.BlockSpec((1, half), lambda i: (0, 0))],
        out_shape=[jax.ShapeDtypeStruct((2, seq, half), F32), jax.ShapeDtypeStruct((1, half), F32)],
        compiler_params=_params(("arbitrary",), VMEM_LIMIT),
        name="hyena_filter",
    )(z2, fw["w1"], fw["b1"], fw["sf"], fw["w2"], fw["b2"], fw["w3a"], fw["w3b"], dl)
    return k.reshape(2 * seq, half), nrm


def _dft_tables(seq):
    assert 2 * seq == DFT * DFT
    n_a = seq // DFT
    idx = np.arange(DFT)
    ang = 2.0 * np.pi * ((idx[:, None] * idx[None, :]) % DFT) / DFT
    fr, fi = jnp.asarray(np.cos(ang), F32), jnp.asarray(-np.sin(ang), F32)
    tang = 2.0 * np.pi * (idx[:, None] * idx[None, :]) / (DFT * DFT)
    tr, ti = jnp.asarray(np.cos(tang), F32), jnp.asarray(-np.sin(tang), F32)
    c = tr[:, :, None] * fr[None] - ti[:, :, None] * fi[None]
    sn = tr[:, :, None] * fi[None] + ti[:, :, None] * fr[None]
    block = lambda p, q, r, s, ax: jnp.concatenate([jnp.concatenate([p, q], axis=ax + 1),
                                                    jnp.concatenate([r, s], axis=ax + 1)], axis=ax).astype(BF16)
    cd, sd = c[:, :, :n_a], sn[:, :, :n_a]
    ct, st = jnp.swapaxes(cd, 1, 2), jnp.swapaxes(sd, 1, 2)
    return {
        "fwd": block(fr, -fi, fi, fr, 0), "inv": block(fr, fi, -fi, fr, 0),
        "s1_real": jnp.concatenate([c, sn], axis=1).astype(BF16),
        "s1": block(cd, -sd, sd, cd, 1),
        "s3": block(ct, st, -st, ct, 1),
    }


def _cmul(ar, ai, br, bi):
    return ar * br - ai * bi, ar * bi + ai * br


def _fft1_kernel(x_ref, f_ref, or_ref, oi_ref, *, n_b, has_im):
    if has_im:
        x = jnp.concatenate([jnp.swapaxes(x_ref[0], 0, 1), jnp.swapaxes(x_ref[1], 0, 1)], axis=1)
    else:
        x = jnp.swapaxes(x_ref[...], 0, 1)
    outs_r, outs_i = [], []
    for j in range(n_b):
        o = _dot(f_ref[j], x[j].astype(BF16))
        outs_r.append(o[:DFT].astype(BF16))
        outs_i.append(o[DFT:].astype(BF16))
    or_ref[...] = jnp.swapaxes(jnp.stack(outs_r, axis=0), 0, 1)
    oi_ref[...] = jnp.swapaxes(jnp.stack(outs_i, axis=0), 0, 1)


def _fft1(x, f1, cw, has_im, n_b=16):
    g = x.shape[0]
    n_a = x.shape[-3]
    if has_im:
        xspec = pl.BlockSpec((None, 2, n_a, n_b, cw), lambda j, i: (i, 0, 0, j, 0))
    else:
        xspec = pl.BlockSpec((None, n_a, n_b, cw), lambda j, i: (i, 0, j, 0))
    ospec = pl.BlockSpec((None, DFT, n_b, cw), lambda j, i: (i, 0, j, 0))
    oshape = jax.ShapeDtypeStruct((g, DFT, DFT, cw), BF16)
    return pl.pallas_call(
        functools.partial(_fft1_kernel, n_b=n_b, has_im=has_im),
        grid=(DFT // n_b, g),
        in_specs=[xspec, pl.BlockSpec((n_b,) + f1.shape[1:], lambda j, i: (j, 0, 0))],
        out_specs=[ospec, ospec],
        out_shape=[oshape, oshape],
        compiler_params=_params(("arbitrary", "arbitrary"), VMEM_LIMIT),
        name="fft_stage1",
    )(x, f1)


def _fft2_filter_kernel(ar_ref, ai_ref, f_ref, nrm_ref, kr_ref, ki_ref, *, qb):
    f = f_ref[...]
    scale = 1.0 / (nrm_ref[...] * float(DFT * DFT))
    for qi in range(qb):
        o = _dot(f, jnp.concatenate([ar_ref[qi], ai_ref[qi]], axis=0))
        kr_ref[qi] = (o[:DFT] * scale).astype(BF16)
        ki_ref[qi] = (o[DFT:] * scale).astype(BF16)


def _fft2_filter(ar, ai, fwd, nrm, cw, qb=4):
    spec = pl.BlockSpec((qb, DFT, cw), lambda i: (i, 0, 0))
    oshape = jax.ShapeDtypeStruct((DFT, DFT, cw), BF16)
    return pl.pallas_call(
        functools.partial(_fft2_filter_kernel, qb=qb),
        grid=(DFT // qb,),
        in_specs=[spec, spec, _full(fwd.shape), _full((1, cw))],
        out_specs=[spec, spec],
        out_shape=[oshape, oshape],
        compiler_params=_params(("arbitrary",), VMEM_LIMIT),
        name="fft_filter_stage2",
    )(ar.reshape(DFT, DFT, cw), ai.reshape(DFT, DFT, cw), fwd, nrm)


def _fft2_kernel(ar_ref, ai_ref, fwd_ref, inv_ref, kr_ref, ki_ref, br_ref, bi_ref, *, qb, groups, cw):
    fwd, inv = fwd_ref[...], inv_ref[...]
    for qi in range(qb):
        a = jnp.concatenate([jnp.concatenate([ar_ref[g, qi], ai_ref[g, qi]], axis=0) for g in range(groups)], axis=1)
        x = _dot(fwd, a)
        kr = jnp.concatenate([kr_ref[qi]] * groups, axis=1)
        ki = jnp.concatenate([ki_ref[qi]] * groups, axis=1)
        yr, yi = _cmul(x[:DFT], x[DFT:], kr, ki)
        o = _dot(inv, jnp.concatenate([yr, yi], axis=0).astype(BF16))
        for g in range(groups):
            br_ref[g, qi] = o[:DFT, g * cw:(g + 1) * cw].astype(BF16)
            bi_ref[g, qi] = o[DFT:, g * cw:(g + 1) * cw].astype(BF16)


def _fft2(ar, ai, fwd, inv, kr, ki, order, cw, qb=8):
    g = ar.shape[0]
    a4 = pl.BlockSpec((g, qb, DFT, cw), lambda i: (0, i, 0, 0))
    ksp = pl.BlockSpec((qb, DFT, cw), lambda i: (i, 0, order))
    oshape = jax.ShapeDtypeStruct((g, DFT, DFT, cw), BF16)
    return pl.pallas_call(
        functools.partial(_fft2_kernel, qb=qb, groups=g, cw=cw),
        grid=(DFT // qb,),
        in_specs=[a4, a4, _full(fwd.shape), _full(inv.shape), ksp, ksp],
        out_specs=[a4, a4],
        out_shape=[oshape, oshape],
        compiler_params=_params(("arbitrary",), VMEM_LIMIT),
        name="fft_stage2",
    )(ar, ai, fwd, inv, kr, ki)


def _fft3_kernel(br_ref, bi_ref, f_ref, z_ref, gate_ref, bias_ref, o_ref, *, n_b, n_a):
    b = jnp.concatenate([jnp.swapaxes(br_ref[...], 0, 1), jnp.swapaxes(bi_ref[...], 0, 1)], axis=1)
    y0, y1 = [], []
    for j in range(n_b):
        o = _dot(f_ref[j], b[j])
        y0.append(o[:n_a])
        y1.append(o[n_a:])
    bias = bias_ref[...]
    for s, ys in enumerate((y0, y1)):
        y = jnp.swapaxes(jnp.stack(ys, axis=0), 0, 1)
        o_ref[s] = gate_ref[s] * (y + z_ref[s] * bias)


def _fft3(br, bi, f3, z, gate, bias, cw, n_b=16):
    g = br.shape[0]
    n_a = f3.shape[1] // 2
    bsp = pl.BlockSpec((None, DFT, n_b, cw), lambda j, i: (i, 0, j, 0))
    zsp = pl.BlockSpec((None, 2, n_a, n_b, cw), lambda j, i: (i, 0, 0, j, 0))
    return pl.pallas_call(
        functools.partial(_fft3_kernel, n_b=n_b, n_a=n_a),
        grid=(DFT // n_b, g),
        in_specs=[bsp, bsp, pl.BlockSpec((n_b,) + f3.shape[1:], lambda j, i: (j, 0, 0)), zsp, zsp, _full((1, cw))],
        out_specs=zsp,
        out_shape=jax.ShapeDtypeStruct(z.shape, F32),
        compiler_params=_params(("arbitrary", "arbitrary"), VMEM_LIMIT),
        name="fft_stage3",
    )(br, bi, f3, z, gate, bias)


def _hyena_long(x1, x2, v, fw, bias, tables):
    b, seq, w = v.shape
    assert b % 2 == 0
    n_a = seq // DFT
    kf, nrm = _hyena_filter(seq, fw)
    half = HY_ORDER * w
    far, fai = _fft1(kf.reshape(1, DFT, DFT, half), tables["s1_real"], half, False)
    kr, ki = _fft2_filter(far, fai, tables["fwd"], nrm, half)
    view = lambda a: a.reshape(b // 2, 2, n_a, DFT, w)
    z, gates = view(v), (view(x1), view(x2))
    for o in range(HY_ORDER):
        ar, ai = _fft1(z, tables["s1"], w, True)
        br, bi = _fft2(ar, ai, tables["fwd"], tables["inv"], kr, ki, o, w)
        z = _fft3(br, bi, tables["s3"], z, gates[o], bias[o:o + 1], w)
    return z.reshape(b, seq, w)


def _ctx_hyena_kernel(x1_ref, x2_ref, v_ref, kf_ref, nrm_ref, fd_ref, fi_ref, bias_ref, o_ref, *, seq, w):
    n = 2 * seq
    fd = fd_ref[...]
    kc = _dot(fd, kf_ref[...].astype(BF16))
    scale = 1.0 / (nrm_ref[...] * float(n))
    kr, ki = kc[:n] * scale, kc[n:] * scale
    finv = fi_ref[...]
    z = v_ref[...]
    gates = (x1_ref[...], x2_ref[...])
    for o in range(HY_ORDER):
        sl = slice(o * w, (o + 1) * w)
        xf = _dot(fd[:, :seq], z.astype(BF16))
        yr, yi = _cmul(xf[:n], xf[n:], kr[:, sl], ki[:, sl])
        y = _dot(finv, jnp.concatenate([yr, yi], axis=0).astype(BF16))
        z = gates[o] * (y + z * bias_ref[o:o + 1, :])
    o_ref[...] = z


def _hyena_ctx(x1, x2, v, fw, bias):
    b, seq, w = v.shape
    n = 2 * seq
    idx = np.arange(n)
    ang = 2.0 * np.pi * ((idx[:, None] * idx[None, :]) % n) / n
    fr, fi = np.cos(ang), -np.sin(ang)
    fd = jnp.asarray(np.concatenate([fr, fi], axis=0), F32).astype(BF16)
    finv = jnp.asarray(np.concatenate([fr[:seq], fi[:seq]], axis=1), F32).astype(BF16)
    kf, nrm = _hyena_filter(seq, fw)
    tok = pl.BlockSpec((None, seq, w), lambda i: (i, 0, 0))
    return pl.pallas_call(
        functools.partial(_ctx_hyena_kernel, seq=seq, w=w),
        grid=(b,),
        in_specs=[tok, tok, tok, _full(kf.shape), _full(nrm.shape), _full(fd.shape), _full(finv.shape),
                  _full(bias.shape)],
        out_specs=tok,
        out_shape=jax.ShapeDtypeStruct(v.shape, F32),
        compiler_params=_params(("arbitrary",), VMEM_LIMIT),
        name="hyena_ctx",
    )(x1, x2, v, kf, nrm, fd, finv, bias)


def _ret_kernel(*refs, reverse, finalize, n_chunk):
    if finalize:
        q_ref, k_ref, v_ref, ld_ref, s0_ref, of_ref, g_ref, avg_ref, out_ref, sfin_ref, s_scr = refs
    else:
        q_ref, k_ref, v_ref, ld_ref, s0_ref, out_ref, sfin_ref, s_scr = refs
    c = q_ref.shape[0] // n_chunk
    t = pl.program_id(1)

    @pl.when(t == 0)
    def _():
        s_scr[...] = s0_ref[...]

    r = lax.broadcasted_iota(jnp.int32, (c, c), 0)
    m = lax.broadcasted_iota(jnp.int32, (c, c), 1)
    diff = ((m - r) if reverse else (r - m)).astype(F32)
    pos = lax.broadcasted_iota(jnp.int32, (c, RET_DV), 0).astype(F32)
    order = range(n_chunk - 1, -1, -1) if reverse else range(n_chunk)
    for hd in range(RET_HEADS):
        lg = jnp.log1p(-jnp.exp(ld_ref[hd]))
        lgv = lg[:, :RET_DV]
        lgc = jnp.concatenate([lg] * (c // LANE), axis=1)
        decay = jnp.where(diff >= 0, jnp.exp(lgc * jnp.maximum(diff, 0.0)), 0.0)
        if reverse:
            zeta = jnp.exp(lgv * pos)
            xi = jnp.exp(lgv * (c - pos))
        else:
            zeta = jnp.exp(lgv * (c - 1 - pos))
            xi = jnp.exp(lgv * (pos + 1))
        g_chunk = jnp.exp(lgv * c)
        hs = slice(hd * RET_DK, (hd + 1) * RET_DK)
        state = s_scr[hd]
        entering = {}
        for ci in order:
            rows = slice(ci * c, (ci + 1) * c)
            kv = lax.dot_general(k_ref[rows, hs].astype(BF16), (v_ref[rows, hs] * zeta).astype(BF16),
                                 (((0,), (0,)), ((), ())), preferred_element_type=F32)
            entering[ci] = state
            state = g_chunk * state + kv
        s_scr[hd] = state
        for ci in order:
            rows = slice(ci * c, (ci + 1) * c)
            qh = q_ref[rows, hs].astype(BF16)
            kh = k_ref[rows, hs].astype(BF16)
            s = lax.dot_general(qh, kh, (((1,), (1,)), ((), ())), preferred_element_type=F32)
            inner = _dot((s * decay).astype(BF16), v_ref[rows, hs].astype(BF16))
            o = inner + _dot(qh, entering[ci].astype(BF16)) * xi
            out_ref[rows, hs] = o

    if finalize:
        o = out_ref[...] + of_ref[...]
        ms = _dot((o * o).astype(BF16), avg_ref[...])
        out_ref[...] = jax.nn.silu(g_ref[...]) * (o * lax.rsqrt(ms + EPS))

    @pl.when(t == pl.num_programs(1) - 1)
    def _():
        sfin_ref[...] = s_scr[...]


def _ret_sweep(q, k, v, ld, s0, reverse, fwd_out=None, gate=None):
    b, t, w = q.shape
    tt = min(t, 1024)
    nt = t // tt
    finalize = fwd_out is not None
    tmap = (lambda i, j: (i, nt - 1 - j, 0)) if reverse else (lambda i, j: (i, j, 0))
    tok = pl.BlockSpec((None, tt, w), tmap)
    st = pl.BlockSpec((None, RET_HEADS, RET_DK, RET_DV), lambda i, j: (i, 0, 0, 0))
    ins = [q, k, v, ld, s0]
    in_specs = [tok, tok, tok, _full(ld.shape), st]
    if finalize:
        avg = jnp.kron(jnp.eye(RET_HEADS, dtype=F32), jnp.full((RET_DV, RET_DV), 1.0 / RET_DV, F32)).astype(BF16)
        ins += [fwd_out, gate, avg]
        in_specs += [tok, tok, _full(avg.shape)]
    return pl.pallas_call(
        functools.partial(_ret_kernel, reverse=reverse, finalize=finalize, n_chunk=tt // RET_KCHUNK),
        grid=(b, nt),
        in_specs=in_specs,
        out_specs=[tok, st],
        out_shape=[jax.ShapeDtypeStruct((b, t, w), F32),
                   jax.ShapeDtypeStruct((b, RET_HEADS, RET_DK, RET_DV), F32)],
        scratch_shapes=[pltpu.VMEM((RET_HEADS, RET_DK, RET_DV), F32)],
        compiler_params=_params(("arbitrary", "arbitrary"), VMEM_LIMIT),
        name="retention_bwd" if reverse else "retention_fwd",
    )(*ins)


def _retention(q, k, v, g, ld, s_f, s_b):
    out_f, fin_f = _ret_sweep(q, k, v, ld[0], s_f, False)
    out, fin_b = _ret_sweep(q, k, v, ld[1], s_b, True, out_f, g)
    return out, fin_f, fin_b


def _attn_kernel(*refs, n_src, tk):
    q_ref = refs[0]
    o_ref, s_buf, p_buf = refs[1 + 2 * n_src:]
    tq = q_ref.shape[0]
    chunks = []
    for si in range(n_src):
        k_ref, v_ref = refs[1 + 2 * si], refs[2 + 2 * si]
        rows = min(tk, k_ref.shape[0])
        chunks += [(k_ref, v_ref, r0, rows) for r0 in range(0, k_ref.shape[0], rows)]
    n = len(chunks)
    for hh in range(2):
        hs = slice(hh * HEAD_PAD, (hh + 1) * HEAD_PAD)
        q = q_ref[:, hs]

        def scores(c, slot):
            k_ref, _, r0, rows = chunks[c]
            s_buf[slot, :, :rows] = lax.dot_general(q, k_ref[r0:r0 + rows, hs], (((1,), (1,)), ((), ())),
                                                    preferred_element_type=F32)

        def pv(c, slot):
            _, v_ref, r0, rows = chunks[c]
            return _dot(p_buf[slot, :, :rows], v_ref[r0:r0 + rows, hs])

        scores(0, 0)
        m = jnp.full((tq, 1), -jnp.inf, F32)
        acc = jnp.zeros((tq, HEAD_PAD), F32)
        a_prev = None
        for c in range(n):
            slot = c % 2
            if c + 1 < n:
                scores(c + 1, 1 - slot)
            if c >= 1:
                acc = a_prev * acc + pv(c - 1, 1 - slot)
            rows = chunks[c][3]
            s = s_buf[slot, :, :rows]
            m_new = jnp.maximum(m, jnp.max(s, axis=-1, keepdims=True))
            a_prev = jnp.exp2(m - m_new)
            p_buf[slot, :, :rows] = jnp.exp2(s - m_new).astype(BF16)
            m = m_new
        acc = a_prev * acc + pv(n - 1, (n - 1) % 2)
        o_ref[:, hh * MLA_V:(hh + 1) * MLA_V] = (acc[:, :MLA_V] / acc[:, MLA_V:MLA_V + 1]).astype(o_ref.dtype)


def _attention(q, kvs, tq, tk):
    b, t, _ = q.shape
    in_specs = [pl.BlockSpec((None, tq, 2 * HEAD_PAD), lambda i, h, j: (i, j, h))]
    args = [q]
    for k, v in kvs:
        in_specs += [pl.BlockSpec((None, k.shape[1], 2 * HEAD_PAD), lambda i, h, j: (i, 0, h))] * 2
        args += [k, v]
    tk = min(tk, max(k.shape[1] for k, _ in kvs))
    return pl.pallas_call(
        functools.partial(_attn_kernel, n_src=len(kvs), tk=tk),
        grid=(b, MLA_HEADS // 2, t // tq),
        in_specs=in_specs,
        out_specs=pl.BlockSpec((None, tq, 2 * MLA_V), lambda i, h, j: (i, j, h)),
        out_shape=jax.ShapeDtypeStruct((b, t, MLA_VW), BF16),
        scratch_shapes=[pltpu.VMEM((2, tq, tk), F32), pltpu.VMEM((2, tq, tk), BF16)],
        compiler_params=_params(("arbitrary", "arbitrary", "arbitrary"), VMEM_LIMIT),
        name="mla_attention",
    )(*args)


def _outmlp_kernel(*refs, final, ffc):
    if final:
        (x_ref, hy_ref, ret_ref, att_ref, ga1_ref, sh2_ref, sc2_ref, ga2_ref, g2_ref, wo_ref, w1_ref, w2_ref,
         fg_ref, o_ref) = refs
    else:
        (x_ref, hy_ref, ret_ref, att_ref, ga1_ref, sh2_ref, sc2_ref, ga2_ref, g2_ref, wo_ref, w1_ref, w2_ref,
         o_ref) = refs
    w = HY_WIDTH
    mix = (_dot(hy_ref[...].astype(BF16), wo_ref[0:w, :]) + _dot(ret_ref[...].astype(BF16), wo_ref[w:2 * w, :])
           + _dot(att_ref[...], wo_ref[2 * w:, :]))
    x = x_ref[...] + ga1_ref[...] * mix
    h = (_rms(x, g2_ref[...]) * (1 + sc2_ref[...]) + sh2_ref[...]).astype(BF16)
    acc = jnp.zeros(x.shape, F32)
    for c0 in range(0, D_FF, ffc):
        hid = jnp.square(jax.nn.relu(_dot(h, w1_ref[:, c0:c0 + ffc]))).astype(BF16)
        acc = acc + _dot(hid, w2_ref[c0:c0 + ffc, :])
    x = x + ga2_ref[...] * acc
    if final:
        x = _rms(x, fg_ref[...])
    o_ref[...] = x


def _outmlp(x, hy, ret, att, ga1, sh2, sc2, ga2, lw, final_g, tm):
    b, t, d = x.shape
    tok = lambda n: pl.BlockSpec((None, tm, n), lambda i, j: (i, j, 0))
    mod = pl.BlockSpec((None, 1, d), lambda i, j: (i, 0, 0))
    final = final_g is not None
    ins = [x, hy, ret, att, ga1, sh2, sc2, ga2, lw["norm2_g"], lw["w_out"], lw["w1"], lw["w2"]]
    in_specs = [tok(d), tok(hy.shape[-1]), tok(ret.shape[-1]), tok(att.shape[-1]), mod, mod, mod, mod,
                _full((1, d)), _full(lw["w_out"].shape), _full(lw["w1"].shape), _full(lw["w2"].shape)]
    if final:
        ins.append(final_g)
        in_specs.append(_full((1, d)))
    return pl.pallas_call(
        functools.partial(_outmlp_kernel, final=final, ffc=1024),
        grid=(b, t // tm),
        in_specs=in_specs,
        out_specs=tok(d),
        out_shape=jax.ShapeDtypeStruct((b, t, d), F32),
        compiler_params=_params(("arbitrary", "arbitrary"), VMEM_LIMIT),
        name="outproj_mlp",
    )(*ins)


def _rope_tables(t):
    rows = t // GRID_W
    row = np.repeat(np.arange(rows), GRID_W).astype(np.float64)
    col = np.tile(np.arange(GRID_W), rows).astype(np.float64)
    n_freq = MLA_ROPE // 4
    inv = ROPE_BASE ** (-np.arange(n_freq) / n_freq)
    ang_r = row[:, None] * inv
    ang_c = col[:, None] * inv
    cr, sr, cc, sc = np.cos(ang_r), np.sin(ang_r), np.cos(ang_c), np.sin(ang_c)
    one = np.ones((t, MLA_NOPE))
    z64 = np.zeros((t, MLA_NOPE))
    z8 = np.zeros((t, n_freq))
    tail1 = np.ones((t, HEAD_PAD - MLA_QK))
    tail0 = np.zeros((t, HEAD_PAD - MLA_QK))
    c = np.concatenate([one, cr, cr, cc, cc, tail1], axis=1)
    s1 = np.concatenate([z64, -sr, z8, -sc, z8, tail0], axis=1)
    s2 = np.concatenate([z64, z8, sr, z8, sc, tail0], axis=1)
    return tuple(jnp.asarray(a, F32) for a in (c, s1, s2))


def _identity_rope_tables(t):
    return jnp.ones((t, HEAD_PAD), F32), jnp.zeros((t, HEAD_PAD), F32), jnp.zeros((t, HEAD_PAD), F32)


def _layer_weights(p):
    d = D_MODEL
    w_in = p["w_in"]
    hy_cols = 3 * HY_WIDTH
    ret_cols = 4 * RET_W
    main = hy_cols + ret_cols + MLA_Q_LORA + MLA_KV_LORA
    w_in_pad = jnp.zeros((d, IN_PAD), BF16).at[:, :main].set(w_in[:, :main].astype(BF16))
    w_in_pad = w_in_pad.at[:, main + MLA_NOPE:main + MLA_QK].set(w_in[:, main:].astype(BF16))
    w_uq = p["mla_w_uq"].reshape(MLA_Q_LORA, MLA_HEADS, MLA_QK)
    w_uq = jnp.pad(w_uq, ((0, 0), (0, 0), (0, HEAD_PAD - MLA_QK))).reshape(MLA_Q_LORA, MLA_QW).astype(BF16)
    w_ukv = p["mla_w_ukv"].reshape(MLA_KV_LORA, MLA_HEADS, MLA_NOPE + MLA_V)
    w_uk = jnp.pad(w_ukv[:, :, :MLA_NOPE], ((0, 0), (0, 0), (0, HEAD_PAD - MLA_NOPE)))
    w_uk = w_uk.reshape(MLA_KV_LORA, MLA_QW).astype(BF16)
    w_uv = jnp.pad(w_ukv[:, :, MLA_NOPE:], ((0, 0), (0, 0), (0, HEAD_PAD - MLA_V)))
    w_uv = w_uv.reshape(MLA_KV_LORA, MLA_QW).astype(BF16)
    f = HY_FFN
    half = HY_ORDER * HY_WIDTH
    two = lambda a: jnp.concatenate([a, a], axis=-1)
    w1 = jnp.zeros((2 * LANE, LANE), F32)
    w1 = w1.at[:HY_EMB, :f].set(p["hy_ffn_w1"]).at[LANE:LANE + HY_EMB, f:].set(p["hy_ffn_w1"])
    w2 = jnp.zeros((LANE, LANE), F32).at[:f, :f].set(p["hy_ffn_w2"]).at[f:, f:].set(p["hy_ffn_w2"])
    fw = {
        "w1": w1, "b1": two(p["hy_ffn_b1"])[None, :], "sf": two(p["hy_sin_freq"]),
        "w2": w2, "b2": two(p["hy_ffn_b2"])[None, :],
        "w3a": jnp.pad(p["hy_ffn_w3"], ((0, f), (0, 0))),
        "w3b": jnp.pad(p["hy_ffn_w3"][:, half:], ((f, 0), (0, 0))),
    }
    ld = jnp.broadcast_to(p["ret_log_decay"][:, :, None, None], (2, RET_HEADS, 1, LANE))
    return {
        "w_in": w_in_pad, "w_uq": w_uq, "w_uk": w_uk, "w_uv": w_uv,
        "q_g": p["mla_q_norm_g"][None, :], "kv_g": p["mla_kv_norm_g"][None, :],
        "norm1_g": p["norm1_g"][None, :], "norm2_g": p["norm2_g"][None, :],
        "w_out": p["w_out"].astype(BF16), "w1": p["mlp_w1"].astype(BF16), "w2": p["mlp_w2"].astype(BF16),
        "conv_w": p["hy_conv_w"], "conv_b": p["hy_conv_b"][None, :], "hy_bias": p["hy_bias"],
        "filt": fw, "ld": ld,
    }


def _mixers(x, sh1, sc1, lw, rope_tabs, tm):
    x1, x2, v, rq, rk, rv, rg, mq, mk, mv = _inproj(x, sh1, sc1, lw["norm1_g"], lw, rope_tabs, tm)
    return (x1, x2, v), (rq, rk, rv, rg), (mq, mk, mv)


def kernel(x, c, ctx, c_ctx, w_ada, b_ada, norm1_g, norm2_g, w_in, w_out, hy_conv_w, hy_conv_b, hy_ffn_w1,
           hy_ffn_b1, hy_sin_freq, hy_ffn_w2, hy_ffn_b2, hy_ffn_w3, hy_bias, ret_log_decay, mla_q_norm_g, mla_w_uq,
           mla_kv_norm_g, mla_w_ukv, mlp_w1, mlp_w2, final_norm_g):
    b, seq, d = x.shape
    n_ctx = ctx.shape[1]
    depth = w_ada.shape[0]
    cond8 = jnp.concatenate([c, c_ctx[None, :], jnp.zeros((8 - b - 1, d), F32)], axis=0)
    ada = _ada(cond8, w_ada, b_ada)
    rope_lat = _rope_tables(seq)
    rope_ctx = _identity_rope_tables(n_ctx)
    dft_tables = _dft_tables(seq)
    zero_state = jnp.zeros((b, RET_HEADS, RET_DK, RET_DV), F32)
    xc = ctx
    for i in range(depth):
        p = {
            "w_in": w_in[i], "w_out": w_out[i], "norm1_g": norm1_g[i], "norm2_g": norm2_g[i],
            "hy_conv_w": hy_conv_w[i], "hy_conv_b": hy_conv_b[i], "hy_ffn_w1": hy_ffn_w1[i],
            "hy_ffn_b1": hy_ffn_b1[i], "hy_sin_freq": hy_sin_freq[i], "hy_ffn_w2": hy_ffn_w2[i],
            "hy_ffn_b2": hy_ffn_b2[i], "hy_ffn_w3": hy_ffn_w3[i], "hy_bias": hy_bias[i],
            "ret_log_decay": ret_log_decay[i], "mla_q_norm_g": mla_q_norm_g[i], "mla_w_uq": mla_w_uq[i],
            "mla_kv_norm_g": mla_kv_norm_g[i], "mla_w_ukv": mla_w_ukv[i], "mlp_w1": mlp_w1[i], "mlp_w2": mlp_w2[i],
        }
        lw = _layer_weights(p)
        last = i == depth - 1
        terms = [ada[i, :, k * d:(k + 1) * d] for k in range(N_MOD)]
        lat = [tm[:b, None, :] for tm in terms]
        cx = [jnp.broadcast_to(tm[b:b + 1, None, :], (b, 1, d)) for tm in terms]

        hy_c, ret_c, (cq, ck, cv) = _mixers(xc, cx[0], cx[1], lw, rope_ctx, n_ctx)
        hy_l, ret_l, (mq, mk, mv) = _mixers(x, lat[0], lat[1], lw, rope_lat, 1024)

        rq, rk, rv, rg = ret_c
        ret_ctx, s_f, s_b = _retention(rq, rk, rv, rg, lw["ld"], zero_state, zero_state)
        rq, rk, rv, rg = ret_l
        ret, _, _ = _retention(rq, rk, rv, rg, lw["ld"], s_f, s_b)

        hy = _hyena_long(*hy_l, lw["filt"], lw["hy_bias"], dft_tables)

        att = _attention(mq, [(mk, mv), (ck, cv)], 1024, ATTN_TK)

        fg = final_norm_g[None, :] if last else None
        x = _outmlp(x, hy, ret, att, lat[2], lat[3], lat[4], lat[5], lw, fg, 512)
        if not last:
            hyc = _hyena_ctx(*hy_c, lw["filt"], lw["hy_bias"])
            att_c = _attention(cq, [(ck, cv)], n_ctx, ATTN_TK)
            xc = _outmlp(xc, hyc, ret_ctx, att_c, cx[2], cx[3], cx[4], cx[5], lw, None, n_ctx)
    return x
```
